```python
import math
import jax, jax.numpy as jnp
from jax import lax
import numpy as np


D_MODEL = 4096
BATCH = 8
SEQ = 2048
DEPTH = 2
DEC_BATCH = 8
DEC_SEQ = 32
PAST_LEN = 1024

CHUNK = 64
N_A_LAYERS = DEPTH // 2
N_B_LAYERS = DEPTH - N_A_LAYERS
SSM_EXPAND = 2
D_INNER = SSM_EXPAND * D_MODEL
SSM_HEAD_DIM = 64
SSM_HEADS = D_INNER // SSM_HEAD_DIM
SSM_GROUPS = 8
SSM_HEADS_PER_GROUP = SSM_HEADS // SSM_GROUPS
SSM_STATE = 128
CONV_W = 4
CONV_DIM = D_INNER + 2 * SSM_GROUPS * SSM_STATE
D_IN_PROJ = D_INNER + CONV_DIM + SSM_HEADS
FOX_HEAD_DIM = 128
FOX_HEADS = D_MODEL // FOX_HEAD_DIM
Q_BLOCK = 128
D_FF = 11008
NORM_EPS = 1e-5

kernel_name = 'hybrid_ssd_fox_stream_step'


def rmsnorm(x, g):
    x32 = x.astype(jnp.float32)
    y = x32 * lax.rsqrt(jnp.mean(x32 * x32, axis=-1, keepdims=True) + NORM_EPS)
    return (y * g.astype(jnp.float32)).astype(x.dtype)


def swiglu_ffn(x, w_gate, w_up, w_down):
    return (jax.nn.silu(x @ w_gate) * (x @ w_up)) @ w_down


def causal_depthwise_conv(x, buf, w, b):
    T = x.shape[1]
    xp = jnp.concatenate([buf.astype(x.dtype), x], axis=1)
    out = b.astype(x.dtype)
    for k in range(CONV_W):
        out = out + xp[:, k:k + T] * w[k]
    new_buf = xp[:, xp.shape[1] - (CONV_W - 1):]
    return out, new_buf


def ssd_chunked_scan(x, dt, A, Bm, Cm, h0, L):
    Bsz, T = x.shape[0], x.shape[1]
    n_blocks = T // L
    a = dt * A

    def blocks(t):
        return jnp.moveaxis(t.reshape((Bsz, n_blocks, L) + t.shape[2:]), 1, 0)

    causal = jnp.tril(jnp.ones((L, L), dtype=bool))[None, :, :, None, None]

    def step(h, inp):
        xc, dtc, ac, bc, cc = inp
        acum = jnp.cumsum(ac, axis=1)
        seg = acum[:, :, None] - acum[:, None, :]
        decay = jnp.exp(jnp.where(causal, seg, -jnp.inf))
        cb = jnp.einsum('blgn,bsgn->blsg', cc, bc)
        xdt = xc * dtc[..., None]
        y = jnp.einsum('blsg,blsgr,bsgrp->blgrp', cb, decay, xdt)
        y = y + jnp.einsum('blgn,bgrpn->blgrp', cc, h) * jnp.exp(acum)[..., None]
        tail = jnp.exp(acum[:, -1:] - acum)
        h = h * jnp.exp(acum[:, -1])[..., None, None] + jnp.einsum('bsgn,bsgr,bsgrp->bgrpn', bc, tail, xdt)
        return h, y

    h, ys = lax.scan(step, h0, (blocks(x), blocks(dt), blocks(a), blocks(Bm), blocks(Cm)))
    y = jnp.moveaxis(ys, 0, 1).reshape(x.shape)
    return y, h


def mamba2_mixer(u, conv_buf, ssm_state, w_in, conv_w, conv_b, dt_bias, A_log, D_skip, norm_g, w_out):
    Bsz, T, _ = u.shape
    G, R, P, N = SSM_GROUPS, SSM_HEADS_PER_GROUP, SSM_HEAD_DIM, SSM_STATE
    proj = u @ w_in
    z = proj[..., :D_INNER]
    xbc = proj[..., D_INNER:D_INNER + CONV_DIM]
    dt_raw = proj[..., D_INNER + CONV_DIM:]
    xbc, new_buf = causal_depthwise_conv(xbc, conv_buf, conv_w, conv_b)
    xbc = jax.nn.silu(xbc).astype(jnp.float32)
    xs = xbc[..., :D_INNER].reshape(Bsz, T, G, R, P)
    Bm = xbc[..., D_INNER:D_INNER + G * N].reshape(Bsz, T, G, N)
    Cm = xbc[..., D_INNER + G * N:].reshape(Bsz, T, G, N)
    dt = jax.nn.softplus(dt_raw.astype(jnp.float32) + dt_bias.astype(jnp.float32)).reshape(Bsz, T, G, R)
    A = -jnp.exp(A_log.astype(jnp.float32)).reshape(G, R)
    L = CHUNK if T % CHUNK == 0 else T
    h0 = ssm_state.astype(jnp.float32).reshape(Bsz, G, R, P, N)
    y, h = ssd_chunked_scan(xs, dt, A, Bm, Cm, h0, L)
    y = y + D_skip.astype(jnp.float32).reshape(G, R)[:, :, None] * xs
    y = y.reshape(Bsz, T, D_INNER) * jax.nn.silu(z.astype(jnp.float32))
    yg = y.reshape(Bsz, T, G, D_INNER // G)
    yg = yg * lax.rsqrt(jnp.mean(yg * yg, axis=-1, keepdims=True) + NORM_EPS)
    y = (yg.reshape(Bsz, T, D_INNER) * norm_g.astype(jnp.float32)).astype(u.dtype)
    new_state = h.reshape(Bsz, SSM_HEADS, P, N).astype(ssm_state.dtype)
    return y @ w_out, new_buf, new_state


def shared_kv(h, kv_norm, w_kvf, b_f):
    Bsz, T, _ = h.shape
    s = rmsnorm(h, kv_norm) @ w_kvf
    k = s[..., :D_MODEL].reshape(Bsz, T, FOX_HEADS, FOX_HEAD_DIM)
    v = s[..., D_MODEL:2 * D_MODEL].reshape(Bsz, T, FOX_HEADS, FOX_HEAD_DIM)
    logf = jax.nn.log_sigmoid(s[..., 2 * D_MODEL:].astype(jnp.float32) + b_f.astype(jnp.float32))
    return k, v, logf


def fox_block(q, k, v, fq, fk, pos_q, pos_k):
    s = jnp.einsum('bqhd,bkhd->bhqk', q, k).astype(jnp.float32) * (FOX_HEAD_DIM ** -0.5)
    s = s + jnp.swapaxes(fq, 1, 2)[:, :, :, None] - jnp.swapaxes(fk, 1, 2)[:, :, None, :]
    s = jnp.where(pos_k[None, None, None, :] <= pos_q[None, None, :, None], s, -jnp.inf)
    p = jax.nn.softmax(s, axis=-1)
    return jnp.einsum('bhqk,bkhd->bqhd', p.astype(v.dtype), v)


def fox_attention(u, k_all, v_all, f_all, f_q, w_q, w_o):
    Bsz, T, _ = u.shape
    past = k_all.shape[1] - T
    q = (u @ w_q).reshape(Bsz, T, FOX_HEADS, FOX_HEAD_DIM)
    pos_k = jnp.arange(past + T)
    pos_q = past + jnp.arange(T)
    if T % Q_BLOCK == 0:
        nb = T // Q_BLOCK
        qb = jnp.moveaxis(q.reshape(Bsz, nb, Q_BLOCK, FOX_HEADS, FOX_HEAD_DIM), 1, 0)
        fb = jnp.moveaxis(f_q.reshape(Bsz, nb, Q_BLOCK, FOX_HEADS), 1, 0)
        pb = pos_q.reshape(nb, Q_BLOCK)
        o = lax.map(lambda blk: fox_block(blk[0], k_all, v_all, blk[1], f_all, blk[2], pos_k), (qb, fb, pb))
        o = jnp.moveaxis(o, 0, 1).reshape(Bsz, T, D_MODEL)
    else:
        o = fox_block(q, k_all, v_all, f_q, f_all, pos_q, pos_k).reshape(Bsz, T, D_MODEL)
    return o @ w_o


def trunk(x, conv_st, ssm_st, cache_k, cache_v, cache_logf, p):
    h = x
    new_conv, new_ssm = [], []
    k_new = v_new = logf_new = None
    k_all = v_all = f_all = f_q = None
    for i in range(DEPTH):
        h = h + 0.5 * swiglu_ffn(rmsnorm(h, p['norm_ffn'][i, 0]), p['ffn_wg'][i, 0], p['ffn_wu'][i, 0], p['ffn_wd'][i, 0])
        u = rmsnorm(h, p['norm_mix'][i])
        if i < N_A_LAYERS:
            m, cbuf, sst = mamba2_mixer(u, conv_st[i], ssm_st[i], p['a_w_in'][i], p['a_conv_w'][i], p['a_conv_b'][i],
                                        p['a_dt_bias'][i], p['a_A_log'][i], p['a_D'][i], p['a_norm'][i], p['a_w_out'][i])
            new_conv.append(cbuf)
            new_ssm.append(sst)
        else:
            j = i - N_A_LAYERS
            m = fox_attention(u, k_all, v_all, f_all, f_q, p['b_wq'][j], p['b_wo'][j])
        h = h + m
        h = h + 0.5 * swiglu_ffn(rmsnorm(h, p['norm_ffn'][i, 1]), p['ffn_wg'][i, 1], p['ffn_wu'][i, 1], p['ffn_wd'][i, 1])
        if i == N_A_LAYERS - 1:
            k_new, v_new, logf_new = shared_kv(h, p['kv_norm'], p['w_kvf'], p['b_f'])
            k_all = jnp.concatenate([cache_k.astype(k_new.dtype), k_new], axis=1)
            v_all = jnp.concatenate([cache_v.astype(v_new.dtype), v_new], axis=1)
            lf_past = cache_logf.astype(jnp.float32)
            f_past = jnp.cumsum(lf_past, axis=1)
            f_q = jnp.sum(lf_past, axis=1, keepdims=True) + jnp.cumsum(logf_new, axis=1)
            f_all = jnp.concatenate([f_past, f_q], axis=1)
    y = rmsnorm(h, p['norm_final'])
    return y, jnp.stack(new_conv), jnp.stack(new_ssm), k_new, v_new, logf_new


def setup_inputs(seed: int = 0) -> dict:
    key = jax.random.key(seed)
    ks = jax.random.split(key, 32)

    def nrm(k, shape, scale):
        return jax.random.normal(k, shape, jnp.float32) * scale

    x_prompt = nrm(ks[0], (BATCH, SEQ, D_MODEL), 1.0)
    x_sample = nrm(ks[1], (DEC_BATCH, DEC_SEQ, D_MODEL), 1.0)
    state_conv = nrm(ks[2], (N_A_LAYERS, DEC_BATCH, CONV_W - 1, CONV_DIM), 1.0)
    state_ssm = nrm(ks[3], (N_A_LAYERS, DEC_BATCH, SSM_HEADS, SSM_HEAD_DIM, SSM_STATE), 0.1)
    cache_k = nrm(ks[4], (DEC_BATCH, PAST_LEN, FOX_HEADS, FOX_HEAD_DIM), 1.0)
    cache_v = nrm(ks[5], (DEC_BATCH, PAST_LEN, FOX_HEADS, FOX_HEAD_DIM), 1.0)
    cache_logf = jax.nn.log_sigmoid(2.0 + nrm(ks[6], (DEC_BATCH, PAST_LEN, FOX_HEADS), 0.5))
    norm_ffn = 1.0 + nrm(ks[7], (DEPTH, 2, D_MODEL), 0.02)
    ffn_wg = nrm(ks[8], (DEPTH, 2, D_MODEL, D_FF), D_MODEL ** -0.5)
    ffn_wu = nrm(ks[9], (DEPTH, 2, D_MODEL, D_FF), D_MODEL ** -0.5)
    ffn_wd = nrm(ks[10], (DEPTH, 2, D_FF, D_MODEL), D_FF ** -0.5)
    norm_mix = 1.0 + nrm(ks[11], (DEPTH, D_MODEL), 0.02)
    a_w_in = nrm(ks[12], (N_A_LAYERS, D_MODEL, D_IN_PROJ), D_MODEL ** -0.5)
    a_conv_w = nrm(ks[13], (N_A_LAYERS, CONV_W, CONV_DIM), CONV_W ** -0.5)
    a_conv_b = nrm(ks[14], (N_A_LAYERS, CONV_DIM), 0.02)
    dt0 = jnp.exp(jax.random.uniform(ks[15], (N_A_LAYERS, SSM_HEADS), jnp.float32,
                                     minval=math.log(1e-3), maxval=math.log(1e-1)))
    a_dt_bias = dt0 + jnp.log(-jnp.expm1(-dt0))
    a_A_log = jnp.log(jax.random.uniform(ks[16], (N_A_LAYERS, SSM_HEADS), jnp.float32, minval=1.0, maxval=16.0))
    a_D = 1.0 + nrm(ks[17], (N_A_LAYERS, SSM_HEADS), 0.1)
    a_norm = 1.0 + nrm(ks[18], (N_A_LAYERS, D_INNER), 0.02)
    a_w_out = nrm(ks[19], (N_A_LAYERS, D_INNER, D_MODEL), D_INNER ** -0.5)
    kv_norm = 1.0 + nrm(ks[20], (D_MODEL,), 0.02)
    w_kvf = nrm(ks[21], (D_MODEL, 2 * D_MODEL + FOX_HEADS), D_MODEL ** -0.5)
    b_f = 2.0 + nrm(ks[22], (FOX_HEADS,), 0.5)
    b_wq = nrm(ks[23], (N_B_LAYERS, D_MODEL, D_MODEL), D_MODEL ** -0.5)
    b_wo = nrm(ks[24], (N_B_LAYERS, D_MODEL, D_MODEL), D_MODEL ** -0.5)
    norm_final = 1.0 + nrm(ks[25], (D_MODEL,), 0.02)
    return {'x_prompt': x_prompt, 'x_sample': x_sample, 'state_conv': state_conv, 'state_ssm': state_ssm,
            'cache_k': cache_k, 'cache_v': cache_v, 'cache_logf': cache_logf,
            'norm_ffn': norm_ffn, 'ffn_wg': ffn_wg, 'ffn_wu': ffn_wu, 'ffn_wd': ffn_wd, 'norm_mix': norm_mix,
            'a_w_in': a_w_in, 'a_conv_w': a_conv_w, 'a_conv_b': a_conv_b, 'a_dt_bias': a_dt_bias,
            'a_A_log': a_A_log, 'a_D': a_D, 'a_norm': a_norm, 'a_w_out': a_w_out,
            'kv_norm': kv_norm, 'w_kvf': w_kvf, 'b_f': b_f, 'b_wq': b_wq, 'b_wo': b_wo, 'norm_final': norm_final}


def reference(x_prompt, x_sample, state_conv, state_ssm, cache_k, cache_v, cache_logf,
              norm_ffn, ffn_wg, ffn_wu, ffn_wd, norm_mix,
              a_w_in, a_conv_w, a_conv_b, a_dt_bias, a_A_log, a_D, a_norm, a_w_out,
              kv_norm, w_kvf, b_f, b_wq, b_wo, norm_final):
    p = {'norm_ffn': norm_ffn, 'ffn_wg': ffn_wg, 'ffn_wu': ffn_wu, 'ffn_wd': ffn_wd, 'norm_mix': norm_mix,
         'a_w_in': a_w_in, 'a_conv_w': a_conv_w, 'a_conv_b': a_conv_b, 'a_dt_bias': a_dt_bias,
         'a_A_log': a_A_log, 'a_D': a_D, 'a_norm': a_norm, 'a_w_out': a_w_out,
         'kv_norm': kv_norm, 'w_kvf': w_kvf, 'b_f': b_f, 'b_wq': b_wq, 'b_wo': b_wo, 'norm_final': norm_final}
    bp = x_prompt.shape[0]
    zero_conv = jnp.zeros((N_A_LAYERS, bp, CONV_W - 1, CONV_DIM), state_conv.dtype)
    zero_ssm = jnp.zeros((N_A_LAYERS, bp, SSM_HEADS, SSM_HEAD_DIM, SSM_STATE), state_ssm.dtype)
    empty_kv = jnp.zeros((bp, 0, FOX_HEADS, FOX_HEAD_DIM), cache_k.dtype)
    empty_lf = jnp.zeros((bp, 0, FOX_HEADS), cache_logf.dtype)
    y_prompt, p_conv, p_ssm, p_k, p_v, p_logf = trunk(x_prompt, zero_conv, zero_ssm, empty_kv, empty_kv, empty_lf, p)
    y_sample, s_conv, s_ssm, s_k, s_v, s_logf = trunk(x_sample, state_conv, state_ssm, cache_k, cache_v, cache_logf, p)
    return (y_prompt, y_sample, p_conv, p_ssm, p_k, p_v, p_logf, s_conv, s_ssm, s_k, s_v, s_logf)
```

```python
import functools

import numpy as np
import jax
import jax.numpy as jnp
from jax import lax
from jax.experimental import pallas as pl
from jax.experimental.pallas import tpu as pltpu

NORM_EPS = 1e-5
SSD_CHUNK = 64
SSM_HEAD_DIM = 64
CONV_W = 4
LANES = 128
VMEM_LIMIT_BYTES = 56 * 1024 * 1024
BF16 = jnp.bfloat16
F32 = jnp.float32


def _params(*sem):
    return pltpu.CompilerParams(dimension_semantics=sem, vmem_limit_bytes=VMEM_LIMIT_BYTES)


def _tile(n, pref):
    if n <= pref:
        return n
    t = 1 << (pref.bit_length() - 1)
    while n % t:
        t //= 2
    return t


def _sigmoid(x):
    return 1.0 / (1.0 + jnp.exp(-x))


def _silu(x):
    return x * _sigmoid(x)


def _softplus(x):
    return jnp.maximum(x, 0.0) + jnp.log1p(jnp.exp(-jnp.abs(x)))


def _cast_kernel(x_ref, o_ref):
    o_ref[...] = x_ref[...].astype(o_ref.dtype)


def cast_bf16(w, n_cols=None):
    R, N = w.shape
    n_cols = N if n_cols is None else n_cols
    tr = _tile(R, max(8, (1 << 21) // n_cols))
    return pl.pallas_call(
        _cast_kernel,
        grid=(R // tr,),
        in_specs=[pl.BlockSpec((tr, n_cols), lambda i: (i, 0))],
        out_specs=pl.BlockSpec((tr, n_cols), lambda i: (i, 0)),
        out_shape=jax.ShapeDtypeStruct((R, n_cols), BF16),
        compiler_params=_params("parallel"),
        name="cast_bf16",
    )(w)


def _rmsnorm_kernel(x_ref, g_ref, o_ref):
    x = x_ref[...]
    y = x * lax.rsqrt(jnp.mean(x * x, axis=-1, keepdims=True) + NORM_EPS)
    o_ref[...] = (y * g_ref[...]).astype(o_ref.dtype)


def rmsnorm(x, g, out_dtype):
    M, D = x.shape
    tm = _tile(M, 512)
    return pl.pallas_call(
        _rmsnorm_kernel,
        grid=(M // tm,),
        in_specs=[pl.BlockSpec((tm, D), lambda i: (i, 0)),
                  pl.BlockSpec((1, D), lambda i: (0, 0))],
        out_specs=pl.BlockSpec((tm, D), lambda i: (i, 0)),
        out_shape=jax.ShapeDtypeStruct((M, D), out_dtype),
        compiler_params=_params("parallel"),
        name="rmsnorm",
    )(x, g.reshape(1, D))


def _mm_plain_kernel(x_ref, w_ref, *o_refs):
    acc = jnp.dot(x_ref[...], w_ref[...], preferred_element_type=F32)
    for o_ref in o_refs:
        o_ref[...] = acc.astype(o_ref.dtype)


def mm_plain(x, w, out_dtypes, *, n_cols, w_row_blk=0, w_col_off=0, tm=1024, tn=1024, name="mm_plain"):
    M, K = x.shape
    tm, tn = _tile(M, tm), _tile(n_cols, tn)
    assert w_col_off % tn == 0
    cb = w_col_off // tn
    return pl.pallas_call(
        _mm_plain_kernel,
        grid=(M // tm, n_cols // tn),
        in_specs=[pl.BlockSpec((tm, K), lambda i, j: (i, 0)),
                  pl.BlockSpec((K, tn), lambda i, j: (w_row_blk, cb + j))],
        out_specs=[pl.BlockSpec((tm, tn), lambda i, j: (i, j)) for _ in out_dtypes],
        out_shape=[jax.ShapeDtypeStruct((M, n_cols), dt) for dt in out_dtypes],
        compiler_params=_params("parallel", "arbitrary"),
        name=name,
    )(x, w)


def _mm_resid_kernel(x_ref, w_ref, r_ref, o_ref, *, alpha):
    acc = jnp.dot(x_ref[...], w_ref[...], preferred_element_type=F32)
    o_ref[...] = r_ref[...] + alpha * acc


def mm_resid(x, w, resid, alpha, *, w_row_blk=0, tm=512, tn=512, name="mm_resid"):
    M, K = x.shape
    N = w.shape[1]
    tm, tn = _tile(M, tm), _tile(N, tn)
    return pl.pallas_call(
        functools.partial(_mm_resid_kernel, alpha=alpha),
        grid=(M // tm, N // tn),
        in_specs=[pl.BlockSpec((tm, K), lambda i, j: (i, 0)),
                  pl.BlockSpec((K, tn), lambda i, j: (w_row_blk, j)),
                  pl.BlockSpec((tm, tn), lambda i, j: (i, j))],
        out_specs=pl.BlockSpec((tm, tn), lambda i, j: (i, j)),
        out_shape=jax.ShapeDtypeStruct((M, N), F32),
        compiler_params=_params("parallel", "arbitrary"),
        name=name,
    )(x, w, resid)


def _mm_gateup_kernel(x_ref, wg_ref, wu_ref, o_ref):
    x = x_ref[...]
    g = jnp.dot(x, wg_ref[...], preferred_element_type=F32)
    u = jnp.dot(x, wu_ref[...], preferred_element_type=F32)
    o_ref[...] = (_silu(g) * u).astype(o_ref.dtype)


def mm_gateup(x, wg, wu, layer, *, tm=2048, tn=256):
    M, K = x.shape
    F = wg.shape[1]
    tm, tn = _tile(M, tm), _tile(F, tn)
    return pl.pallas_call(
        _mm_gateup_kernel,
        grid=(M // tm, F // tn),
        in_specs=[pl.BlockSpec((tm, K), lambda i, j: (i, 0)),
                  pl.BlockSpec((K, tn), lambda i, j: (layer, j)),
                  pl.BlockSpec((K, tn), lambda i, j: (layer, j))],
        out_specs=pl.BlockSpec((tm, tn), lambda i, j: (i, j)),
        out_shape=jax.ShapeDtypeStruct((M, F), BF16),
        compiler_params=_params("parallel", "arbitrary"),
        name="mm_gateup",
    )(x, wg, wu)


def _mm_logf_kernel(x_ref, w_ref, b_ref, o_ref):
    s = jnp.dot(x_ref[...], w_ref[...], preferred_element_type=F32) + b_ref[...]
    o_ref[...] = -_softplus(-s)


def mm_logf(x, w, b):
    M, K = x.shape
    N = w.shape[1]
    tm = _tile(M, 2048)
    return pl.pallas_call(
        _mm_logf_kernel,
        grid=(M // tm,),
        in_specs=[pl.BlockSpec((tm, K), lambda i: (i, 0)),
                  pl.BlockSpec((K, N), lambda i: (0, 0)),
                  pl.BlockSpec((1, N), lambda i: (0, 0))],
        out_specs=pl.BlockSpec((tm, N), lambda i: (i, 0)),
        out_shape=jax.ShapeDtypeStruct((M, N), F32),
        compiler_params=_params("parallel"),
        name="mm_logf",
    )(x, w, b)


def _split3(v):
    hi = v.astype(BF16)
    r1 = v - hi.astype(F32)
    mid = r1.astype(BF16)
    lo = (r1 - mid.astype(F32)).astype(BF16)
    return [hi, mid, lo]


def _ssd_kernel(z_ref, x_ref, b_ref, c_ref, dt_ref,
                cwx_ref, cwb_ref, cwc_ref, cbx_ref, cbb_ref, cbc_ref,
                dtb_ref, alog_ref, dx_ref, ng_ref, e_ref,
                h0_ref, sx0_ref, sb0_ref, sc0_ref,
                y_ref, hout_ref, sxo_ref, sbo_ref, sco_ref,
                ht_scr, xp_scr, bp_scr, cp_scr, *, L):
    c = pl.program_id(2)
    P = SSM_HEAD_DIM
    GW = x_ref.shape[1]

    @pl.when(c == 0)
    def _():
        ht_scr[...] = h0_ref[0].T
        xp_scr[5:8, :] = sx0_ref[0]
        bp_scr[5:8, :] = sb0_ref[0]
        cp_scr[5:8, :] = sc0_ref[0]

    def conv_silu(v_ref, p_scr, w_ref, bias_ref):
        p_scr[8:8 + L, :] = v_ref[...]
        acc = bias_ref[...] + p_scr[5:5 + L, :] * w_ref[0:1, :]
        for k in range(1, CONV_W):
            acc = acc + p_scr[5 + k:5 + k + L, :] * w_ref[k:k + 1, :]
        p_scr[5:8, :] = p_scr[5 + L:8 + L, :]
        return _silu(acc)

    xs = conv_silu(x_ref, xp_scr, cwx_ref, cbx_ref)
    Bc = conv_silu(b_ref, bp_scr, cwb_ref, cbb_ref)
    Cc = conv_silu(c_ref, cp_scr, cwc_ref, cbc_ref)
    sxo_ref[0] = xp_scr[5:8, :]
    sbo_ref[0] = bp_scr[5:8, :]
    sco_ref[0] = cp_scr[5:8, :]
    Bb = Bc.astype(BF16)
    Cb = Cc.astype(BF16)

    dt = _softplus(dt_ref[...] + dtb_ref[...])
    a = dt * (-jnp.exp(alog_ref[...]))
    ri = lax.broadcasted_iota(jnp.int32, (L, L), 0)
    ci = lax.broadcasted_iota(jnp.int32, (L, L), 1)
    acum = jnp.zeros_like(a)
    for piece in _split3(a):
        acum = acum + jnp.dot((ri >= ci).astype(BF16), piece, preferred_element_type=F32)

    lhs = jnp.concatenate(_split3(acum) + _split3(dt), axis=0)
    ex = jnp.dot(lhs, e_ref[0], preferred_element_type=F32)
    acum_x = ex[0:L] + ex[L:2 * L] + ex[2 * L:3 * L]
    dt_x = ex[3 * L:4 * L] + ex[4 * L:5 * L] + ex[5 * L:6 * L]

    xdt = xs * dt_x
    ht = ht_scr[...]
    y_inter = jnp.dot(Cb, ht.astype(BF16), preferred_element_type=F32) * jnp.exp(acum_x)

    lane = lax.broadcasted_iota(jnp.int32, (L, LANES), 1)
    rowi = lax.broadcasted_iota(jnp.int32, (L, LANES), 0)
    s_idx = jnp.bitwise_and(lane, P - 1)
    r_idx = jnp.right_shift(lane, P.bit_length() - 1)
    eye2 = rowi == s_idx
    tri2 = s_idx <= rowi
    pad = [] if L == P else [jnp.zeros((P - L, LANES), BF16)]
    b2 = jnp.concatenate([Bb] + pad + [Bb] + pad, axis=0)
    cb2 = lax.dot_general(Cb, b2, (((1,), (1,)), ((), ())), preferred_element_type=F32)
    y_parts = []
    for j in range(GW // LANES):
        sl = slice(j * LANES, (j + 1) * LANES)
        ax = acum_x[:, sl]
        v = jnp.sum(jnp.where(eye2, ax, 0.0), axis=0, keepdims=True)
        decay = jnp.exp(jnp.where(tri2, ax - v, -jnp.inf))
        m2 = (cb2 * decay).astype(BF16)
        xp = xdt[:, sl]
        top = jnp.where(r_idx == 0, xp, 0.0).astype(BF16)
        bot = jnp.where(r_idx == 1, xp, 0.0).astype(BF16)
        bd = jnp.concatenate([top] + pad + [bot] + pad, axis=0)
        y_parts.append(jnp.dot(m2, bd, preferred_element_type=F32))
    y = jnp.concatenate(y_parts, axis=1) + y_inter + dx_ref[...] * xs

    y = y * _silu(z_ref[...])
    y = y * lax.rsqrt(jnp.mean(y * y, axis=-1, keepdims=True) + NORM_EPS)
    y_ref[...] = (y * ng_ref[...]).astype(y_ref.dtype)

    last = acum_x[L - 1:L, :]
    xt = (xdt * jnp.exp(last - acum_x)).astype(BF16)
    upd = lax.dot_general(Bb, xt, (((0,), (0,)), ((), ())), preferred_element_type=F32)
    ht_new = ht * jnp.exp(last) + upd
    ht_scr[...] = ht_new

    @pl.when(c == pl.num_programs(2) - 1)
    def _():
        hout_ref[0] = ht_new.T


def ssd_mixer(proj, dt_raw, conv_state, ssm_state, conv_w, conv_b, dt_bias, a_log, d_skip, norm_g,
              *, Bsz, T, d_inner, groups, n_state):
    M = Bsz * T
    heads = dt_raw.shape[1]
    P = SSM_HEAD_DIM
    GW = d_inner // groups
    N = n_state
    R = GW // P
    L = SSD_CHUNK if T % SSD_CHUNK == 0 else T
    nc = T // L
    assert N == LANES and heads == LANES and GW % LANES == 0 and L % 16 == 0 and L <= P
    xb, bb, cb = d_inner // GW, 2 * d_inner // N, (2 * d_inner + groups * N) // N

    expand = np.zeros((groups, heads, GW), np.float32)
    for g in range(groups):
        for r in range(R):
            expand[g, g * R + r, r * P:(r + 1) * P] = 1.0
    expand = jnp.asarray(expand, BF16)

    cw = conv_w
    cbias = conv_b.reshape(1, -1)
    d_x = jnp.repeat(d_skip, P).reshape(1, d_inner)
    h0 = ssm_state.reshape(Bsz, heads * P, N)

    row = lambda b, g, c: b * nc + c
    in_specs = [
        pl.BlockSpec((L, GW), lambda b, g, c: (row(b, g, c), g)),
        pl.BlockSpec((L, GW), lambda b, g, c: (row(b, g, c), xb + g)),
        pl.BlockSpec((L, N), lambda b, g, c: (row(b, g, c), bb + g)),
        pl.BlockSpec((L, N), lambda b, g, c: (row(b, g, c), cb + g)),
        pl.BlockSpec((L, heads), lambda b, g, c: (row(b, g, c), 0)),
        pl.BlockSpec((CONV_W, GW), lambda b, g, c: (0, g)),
        pl.BlockSpec((CONV_W, N), lambda b, g, c: (0, d_inner // N + g)),
        pl.BlockSpec((CONV_W, N), lambda b, g, c: (0, d_inner // N + groups + g)),
        pl.BlockSpec((1, GW), lambda b, g, c: (0, g)),
        pl.BlockSpec((1, N), lambda b, g, c: (0, d_inner // N + g)),
        pl.BlockSpec((1, N), lambda b, g, c: (0, d_inner // N + groups + g)),
        pl.BlockSpec((1, heads), lambda b, g, c: (0, 0)),
        pl.BlockSpec((1, heads), lambda b, g, c: (0, 0)),
        pl.BlockSpec((1, GW), lambda b, g, c: (0, g)),
        pl.BlockSpec((1, GW), lambda b, g, c: (0, g)),
        pl.BlockSpec((1, heads, GW), lambda b, g, c: (g, 0, 0)),
        pl.BlockSpec((1, GW, N), lambda b, g, c: (b, g, 0)),
        pl.BlockSpec((1, CONV_W - 1, GW), lambda b, g, c: (b, 0, g)),
        pl.BlockSpec((1, CONV_W - 1, N), lambda b, g, c: (b, 0, d_inner // N + g)),
        pl.BlockSpec((1, CONV_W - 1, N), lambda b, g, c: (b, 0, d_inner // N + groups + g)),
    ]
    out_specs = [
        pl.BlockSpec((L, GW), lambda b, g, c: (row(b, g, c), g)),
        pl.BlockSpec((1, GW, N), lambda b, g, c: (b, g, 0)),
        pl.BlockSpec((1, CONV_W - 1, GW), lambda b, g, c: (b, 0, g)),
        pl.BlockSpec((1, CONV_W - 1, N), lambda b, g, c: (b, 0, g)),
        pl.BlockSpec((1, CONV_W - 1, N), lambda b, g, c: (b, 0, g)),
    ]
    out_shape = [
        jax.ShapeDtypeStruct((M, d_inner), BF16),
        jax.ShapeDtypeStruct((Bsz, heads * P, N), F32),
        jax.ShapeDtypeStruct((Bsz, CONV_W - 1, d_inner), F32),
        jax.ShapeDtypeStruct((Bsz, CONV_W - 1, groups * N), F32),
        jax.ShapeDtypeStruct((Bsz, CONV_W - 1, groups * N), F32),
    ]
    y, h_new, sx, sb, sc = pl.pallas_call(
        functools.partial(_ssd_kernel, L=L),
        grid=(Bsz, groups, nc),
        in_specs=in_specs,
        out_specs=out_specs,
        out_shape=out_shape,
        scratch_shapes=[pltpu.VMEM((N, GW), F32),
                        pltpu.VMEM((8 + L, GW), F32),
                        pltpu.VMEM((8 + L, N), F32),
                        pltpu.VMEM((8 + L, N), F32)],
        compiler_params=_params("parallel", "parallel", "arbitrary"),
        name="ssd_mixer",
    )(proj, proj, proj, proj, dt_raw, cw, cw, cw, cbias, cbias, cbias,
      dt_bias.reshape(1, heads), a_log.reshape(1, heads), d_x, norm_g.reshape(1, d_inner), expand,
      h0, conv_state, conv_state, conv_state)
    new_conv = jnp.concatenate([sx, sb, sc], axis=-1)
    return y, new_conv, h_new.reshape(Bsz, heads, P, N)


def _cumsum_rows(x):
    n = x.shape[0]
    rowi = lax.broadcasted_iota(jnp.int32, x.shape, 0)
    s = 1
    while s < n:
        x = x + jnp.where(rowi >= s, pltpu.roll(x, s, 0), 0.0)
        s *= 2
    return x


def _fcum_kernel(past_ref, new_ref, o_ref, w_scr, *, past, Tn, heads):
    w_scr[...] = jnp.zeros_like(w_scr)
    if past:
        w_scr[0:past, 0:heads] = past_ref[0]
    w_scr[past:past + Tn, :] = new_ref[...]
    o_ref[0] = _cumsum_rows(w_scr[...]).T


def fcum(cache_logf, logf_new, *, Bsz, Tn):
    past, heads = cache_logf.shape[1], cache_logf.shape[2]
    t_pad = -(-(past + Tn) // LANES) * LANES
    if past == 0:
        cache_logf = jnp.zeros((Bsz, 8, heads), F32)
    return pl.pallas_call(
        functools.partial(_fcum_kernel, past=past, Tn=Tn, heads=heads),
        grid=(Bsz,),
        in_specs=[pl.BlockSpec((1,) + cache_logf.shape[1:], lambda b: (b, 0, 0)),
                  pl.BlockSpec((Tn, LANES), lambda b: (b, 0))],
        out_specs=pl.BlockSpec((1, LANES, t_pad), lambda b: (b, 0, 0)),
        out_shape=jax.ShapeDtypeStruct((Bsz, LANES, t_pad), F32),
        scratch_shapes=[pltpu.VMEM((t_pad, LANES), F32)],
        compiler_params=_params("parallel"),
        name="fcum",
    )(cache_logf, logf_new)


def _row_to_col(frow):
    n = frow.shape[1]
    return jnp.broadcast_to(frow, (LANES, n)).T[:, 0:1]


def _attn_prompt_kernel(q_ref, k_ref, v_ref, fq_ref, fk_ref, o_ref, m_scr, l_scr, acc_scr, *, tq, tk, scale):
    h = pl.program_id(1)
    qi = pl.program_id(2)
    hh = h % 8
    fq_col = _row_to_col(fq_ref[0, pl.ds(hh, 1), :])
    q = q_ref[...]
    m_scr[...] = jnp.full_like(m_scr, -jnp.inf)
    l_scr[...] = jnp.zeros_like(l_scr)
    acc_scr[...] = jnp.zeros_like(acc_scr)
    qpos = qi * tq + lax.broadcasted_iota(jnp.int32, (tq, tk), 0)
    kofs = lax.broadcasted_iota(jnp.int32, (tq, tk), 1)

    def body(j, carry):
        start = pl.multiple_of(j * tk, tk)
        kj = k_ref[pl.ds(start, tk), :]
        vj = v_ref[pl.ds(start, tk), :]
        fk = fk_ref[0, hh, pl.ds(j, 1), :]
        s = lax.dot_general(q, kj, (((1,), (1,)), ((), ())), preferred_element_type=F32) * scale
        s = s + fq_col - fk
        s = jnp.where(j * tk + kofs <= qpos, s, -jnp.inf)
        m_prev = m_scr[...]
        m_new = jnp.maximum(m_prev, jnp.max(s, axis=1, keepdims=True))
        alpha = jnp.exp(m_prev - m_new)
        p = jnp.exp(s - m_new)
        l_scr[...] = alpha * l_scr[...] + jnp.sum(p, axis=1, keepdims=True)
        acc_scr[...] = alpha * acc_scr[...] + jnp.dot(p.astype(BF16), vj, preferred_element_type=F32)
        m_scr[...] = m_new
        return carry

    lax.fori_loop(0, (qi + 1) * (tq // tk), body, 0)
    o_ref[...] = (acc_scr[...] / l_scr[...]).astype(o_ref.dtype)


def attn_prompt(q, k, v, f_t, *, Bsz, T, heads, hd, tq=512, tk=256):
    tq, tk = _tile(T, tq), _tile(T, tk)
    assert tq % tk == 0 and tk % LANES == 0 and heads % 8 == 0
    nq, nk = T // tq, T // tk
    f4 = f_t.reshape(Bsz, LANES, nk, tk)
    return pl.pallas_call(
        functools.partial(_attn_prompt_kernel, tq=tq, tk=tk, scale=hd ** -0.5),
        grid=(Bsz, heads, nq),
        in_specs=[pl.BlockSpec((tq, hd), lambda b, h, i: (b * nq + i, h)),
                  pl.BlockSpec((T, hd), lambda b, h, i: (b, h)),
                  pl.BlockSpec((T, hd), lambda b, h, i: (b, h)),
                  pl.BlockSpec((1, 8, tq), lambda b, h, i: (b, h // 8, i)),
                  pl.BlockSpec((1, 8, nk, tk), lambda b, h, i: (b, h // 8, 0, 0))],
        out_specs=pl.BlockSpec((tq, hd), lambda b, h, i: (b * nq + i, h)),
        out_shape=jax.ShapeDtypeStruct((Bsz * T, heads * hd), BF16),
        scratch_shapes=[pltpu.VMEM((tq, 1), F32), pltpu.VMEM((tq, 1), F32), pltpu.VMEM((tq, hd), F32)],
        compiler_params=_params("parallel", "parallel", "arbitrary"),
        name="attn_prompt",
    )(q, k, v, f_t, f4)


def _attn_sample_kernel(q_ref, kc_ref, vc_ref, kn_ref, vn_ref, f_ref, o_ref, *, Tn, past, hd, nh, scale):
    zrows = jnp.zeros((LANES - Tn, hd), BF16)
    rowi = lax.broadcasted_iota(jnp.int32, (Tn, LANES), 0)
    coli = lax.broadcasted_iota(jnp.int32, (Tn, LANES), 1)
    nt = (((1,), (1,)), ((), ()))
    for hh in range(nh):
        sl = slice(hh * hd, (hh + 1) * hd)
        q = q_ref[:, sl]
        kc = kc_ref[0, :, sl].astype(BF16)
        vc = vc_ref[0, :, sl].astype(BF16)
        kn = jnp.concatenate([kn_ref[:, sl], zrows], axis=0)
        vn = jnp.concatenate([vn_ref[:, sl], zrows], axis=0)
        f_past = f_ref[0, hh:hh + 1, 0:past]
        f_new = f_ref[0, hh:hh + 1, past:past + LANES]
        fq_col = _row_to_col(f_new)[0:Tn]
        s1 = lax.dot_general(q, kc, nt, preferred_element_type=F32) * scale + fq_col - f_past
        s2 = lax.dot_general(q, kn, nt, preferred_element_type=F32) * scale + fq_col - f_new
        s2 = jnp.where(coli <= rowi, s2, -jnp.inf)
        m = jnp.maximum(jnp.max(s1, axis=1, keepdims=True), jnp.max(s2, axis=1, keepdims=True))
        p1 = jnp.exp(s1 - m)
        p2 = jnp.exp(s2 - m)
        l = jnp.sum(p1, axis=1, keepdims=True) + jnp.sum(p2, axis=1, keepdims=True)
        o = jnp.dot(p1.astype(BF16), vc, preferred_element_type=F32)
        o = o + jnp.dot(p2.astype(BF16), vn, preferred_element_type=F32)
        o_ref[:, sl] = (o / l).astype(o_ref.dtype)


def attn_sample(q, cache_k, cache_v, k_new, v_new, f_t, *, Bsz, Tn, heads, hd):
    past = cache_k.shape[1]
    nh = 8
    assert past % LANES == 0 and Tn % 16 == 0 and Tn <= LANES and heads % nh == 0
    ck = cache_k.reshape(Bsz, past, heads * hd)
    cv = cache_v.reshape(Bsz, past, heads * hd)
    return pl.pallas_call(
        functools.partial(_attn_sample_kernel, Tn=Tn, past=past, hd=hd, nh=nh, scale=hd ** -0.5),
        grid=(Bsz, heads // nh),
        in_specs=[pl.BlockSpec((Tn, nh * hd), lambda b, g: (b, g)),
                  pl.BlockSpec((1, past, nh * hd), lambda b, g: (b, 0, g)),
                  pl.BlockSpec((1, past, nh * hd), lambda b, g: (b, 0, g)),
                  pl.BlockSpec((Tn, nh * hd), lambda b, g: (b, g)),
                  pl.BlockSpec((Tn, nh * hd), lambda b, g: (b, g)),
                  pl.BlockSpec((1, nh, f_t.shape[2]), lambda b, g: (b, g, 0))],
        out_specs=pl.BlockSpec((Tn, nh * hd), lambda b, g: (b, g)),
        out_shape=jax.ShapeDtypeStruct((Bsz * Tn, heads * hd), BF16),
        compiler_params=_params("parallel", "parallel"),
        name="attn_sample",
    )(q, ck, cv, k_new, v_new, f_t)


def _trunk(x, conv_st, ssm_st, cache_k, cache_v, cache_logf, w, p):
    Bsz, T, D = x.shape
    M = Bsz * T
    heads, hd = cache_k.shape[2], cache_k.shape[3]
    d_inner = p["a_norm"].shape[-1]
    n_state = ssm_st.shape[-1]
    groups = (conv_st.shape[-1] - d_inner) // (2 * n_state)
    d_zxbc = 2 * d_inner + 2 * groups * n_state

    def ffn(h, idx):
        xn = rmsnorm(h, p["norm_ffn"].reshape(-1, D)[idx], BF16)
        a = mm_gateup(xn, w["ffn_wg"], w["ffn_wu"], idx)
        return mm_resid(a, w["ffn_wd"], h, 0.5, w_row_blk=idx, name="mm_down")

    h = x.reshape(M, D)
    h = ffn(h, 0)
    u = rmsnorm(h, p["norm_mix"][0], BF16)
    (proj,) = mm_plain(u, w["a_w_in"], [F32], n_cols=d_zxbc, name="mm_in_proj")
    (dt_raw,) = mm_plain(u, w["a_w_in"], [F32], n_cols=LANES, w_col_off=d_zxbc, tm=2048, name="mm_dt")
    y, new_conv, new_ssm = ssd_mixer(
        proj, dt_raw, conv_st[0], ssm_st[0], p["a_conv_w"][0], p["a_conv_b"][0], p["a_dt_bias"][0],
        p["a_A_log"][0], p["a_D"][0], p["a_norm"][0],
        Bsz=Bsz, T=T, d_inner=d_inner, groups=groups, n_state=n_state)
    h = mm_resid(y, w["a_w_out"], h, 1.0, name="mm_out_proj")
    h = ffn(h, 1)
    xn = rmsnorm(h, p["kv_norm"], BF16)
    k32, kb = mm_plain(xn, w["w_kv"], [F32, BF16], n_cols=D, name="mm_k")
    v32, vb = mm_plain(xn, w["w_kv"], [F32, BF16], n_cols=D, w_col_off=D, name="mm_v")
    logf = mm_logf(xn, w["w_f"], w["b_f"])
    f_t = fcum(cache_logf, logf, Bsz=Bsz, Tn=T)
    h = ffn(h, 2)
    u = rmsnorm(h, p["norm_mix"][1], BF16)
    (q,) = mm_plain(u, w["b_wq"], [BF16], n_cols=D, name="mm_q")
    if cache_k.shape[1] == 0:
        o = attn_prompt(q, kb, vb, f_t, Bsz=Bsz, T=T, heads=heads, hd=hd)
    else:
        o = attn_sample(q, cache_k, cache_v, kb, vb, f_t, Bsz=Bsz, Tn=T, heads=heads, hd=hd)
    h = mm_resid(o, w["b_wo"], h, 1.0, name="mm_wo")
    h = ffn(h, 3)
    y_out = rmsnorm(h, p["norm_final"], F32).reshape(Bsz, T, D)
    return (y_out, new_conv[None], new_ssm[None],
            k32.reshape(Bsz, T, heads, hd), v32.reshape(Bsz, T, heads, hd),
            logf[:, :heads].reshape(Bsz, T, heads))


def kernel(x_prompt, x_sample, state_conv, state_ssm, cache_k, cache_v, cache_logf, norm_ffn, ffn_wg, ffn_wu,
           ffn_wd, norm_mix, a_w_in, a_conv_w, a_conv_b, a_dt_bias, a_A_log, a_D, a_norm, a_w_out, kv_norm,
           w_kvf, b_f, b_wq, b_wo, norm_final):
    assert norm_mix.shape[0] == 2 and a_w_in.shape[0] == 1 and b_wq.shape[0] == 1
    D = x_prompt.shape[-1]
    F = ffn_wg.shape[-1]
    heads = cache_k.shape[2]
    p = {"norm_ffn": norm_ffn, "norm_mix": norm_mix, "a_conv_w": a_conv_w, "a_conv_b": a_conv_b,
         "a_dt_bias": a_dt_bias, "a_A_log": a_A_log, "a_D": a_D, "a_norm": a_norm, "kv_norm": kv_norm,
         "norm_final": norm_final}
    w_f = jnp.pad(w_kvf[:, 2 * D:], ((0, 0), (0, LANES - heads))).astype(BF16)
    w = {"ffn_wg": cast_bf16(ffn_wg.reshape(-1, F)),
         "ffn_wu": cast_bf16(ffn_wu.reshape(-1, F)),
         "ffn_wd": cast_bf16(ffn_wd.reshape(-1, D)),
         "a_w_in": cast_bf16(a_w_in[0]),
         "a_w_out": cast_bf16(a_w_out[0]),
         "w_kv": cast_bf16(w_kvf, 2 * D),
         "w_f": w_f,
         "b_f": jnp.pad(b_f, (0, LANES - heads)).reshape(1, LANES),
         "b_wq": cast_bf16(b_wq[0]),
         "b_wo": cast_bf16(b_wo[0])}

    bp = x_prompt.shape[0]
    zero_conv = jnp.zeros((state_conv.shape[0], bp) + state_conv.shape[2:], state_conv.dtype)
    zero_ssm = jnp.zeros((state_ssm.shape[0], bp) + state_ssm.shape[2:], state_ssm.dtype)
    empty_kv = jnp.zeros((bp, 0) + cache_k.shape[2:], cache_k.dtype)
    empty_lf = jnp.zeros((bp, 0, heads), cache_logf.dtype)
    y_p, p_conv, p_ssm, p_k, p_v, p_logf = _trunk(x_prompt, zero_conv, zero_ssm, empty_kv, empty_kv, empty_lf, w, p)
    y_s, s_conv, s_ssm, s_k, s_v, s_logf = _trunk(x_sample, state_conv, state_ssm, cache_k, cache_v, cache_logf, w, p)
    return (y_p, y_s, p_conv, p_ssm, p_k, p_v, p_logf, s_conv, s_ssm, s_k, s_v, s_logf)
```

```python
import functools

import numpy as np
import jax
import jax.numpy as jnp
from jax import lax
from jax.experimental import pallas as pl
from jax.experimental.pallas import tpu as pltpu

NORM_EPS = 1e-5
SSD_CHUNK = 64
SSM_HEAD_DIM = 64
CONV_W = 4
LANES = 128
VMEM_LIMIT_BYTES = 56 * 1024 * 1024
SSD_GROUPS_PER_STEP = 2
ATTN_BLOCK = 256
ATTN_HEADS_PER_STEP = 4
BF16 = jnp.bfloat16
F32 = jnp.float32


def _params(*sem):
    return pltpu.CompilerParams(dimension_semantics=sem, vmem_limit_bytes=VMEM_LIMIT_BYTES)


def _tile(n, pref):
    if n <= pref:
        return n
    t = 1 << (pref.bit_length() - 1)
    while n % t:
        t //= 2
    return t


def _sigmoid(x):
    return 1.0 / (1.0 + jnp.exp(-x))


def _silu(x):
    return x * _sigmoid(x)


def _softplus(x):
    return jnp.maximum(x, 0.0) + jnp.log1p(jnp.exp(-jnp.abs(x)))


def _cumsum_rows(x, seg):
    rowi = jnp.bitwise_and(lax.broadcasted_iota(jnp.int32, x.shape, 0), seg - 1)
    s = 1
    while s < seg:
        x = x + jnp.where(rowi >= s, pltpu.roll(x, s, 0), 0.0)
        s *= 2
    return x


def _rmsnorm_kernel(x_ref, g_ref, o_ref):
    x = x_ref[...]
    y = x * lax.rsqrt(jnp.mean(x * x, axis=-1, keepdims=True) + NORM_EPS)
    o_ref[...] = (y * g_ref[...]).astype(o_ref.dtype)


def rmsnorm(x, g, out_dtype):
    M, D = x.shape
    tm = _tile(M, 512)
    return pl.pallas_call(
        _rmsnorm_kernel,
        grid=(M // tm,),
        in_specs=[pl.BlockSpec((tm, D), lambda i: (i, 0)),
                  pl.BlockSpec((1, D), lambda i: (0, 0))],
        out_specs=pl.BlockSpec((tm, D), lambda i: (i, 0)),
        out_shape=jax.ShapeDtypeStruct((M, D), out_dtype),
        compiler_params=_params("parallel"),
        name="rmsnorm",
    )(x, g.reshape(1, D))


def _w_bf16(w_ref, wb_ref):
    w = w_ref[...]
    if wb_ref is not None:
        w = w.astype(BF16)
        wb_ref[...] = w
    return w


def _w_specs(w, K, n_cols, tn, M, tm):
    arr, row_blk, col_off = w
    emit = arr.dtype == F32
    assert col_off % tn == 0 and n_cols % tn == 0
    assert not emit or M == tm
    cb = col_off // tn
    in_spec = pl.BlockSpec((K, tn), lambda i, j: (row_blk, cb + j))
    out_specs = [pl.BlockSpec((K, tn), lambda i, j: (0, j))] if emit else []
    out_shapes = [jax.ShapeDtypeStruct((K, n_cols), BF16)] if emit else []
    return arr, emit, in_spec, out_specs, out_shapes


def _mm_plain_kernel(x_ref, w_ref, *o_refs, emit):
    wb_ref = o_refs[-1] if emit else None
    acc = jnp.dot(x_ref[...], _w_bf16(w_ref, wb_ref), preferred_element_type=F32)
    for o_ref in o_refs[:len(o_refs) - emit]:
        o_ref[...] = acc.astype(o_ref.dtype)


def mm_plain(x, w, out_dtypes, *, n_cols, tm=1024, tn=1024, name="mm_plain"):
    M, K = x.shape
    tm, tn = _tile(M, tm), _tile(n_cols, tn)
    arr, emit, w_spec, wb_specs, wb_shapes = _w_specs(w, K, n_cols, tn, M, tm)
    outs = pl.pallas_call(
        functools.partial(_mm_plain_kernel, emit=emit),
        grid=(M // tm, n_cols // tn),
        in_specs=[pl.BlockSpec((tm, K), lambda i, j: (i, 0)), w_spec],
        out_specs=[pl.BlockSpec((tm, tn), lambda i, j: (i, j)) for _ in out_dtypes] + wb_specs,
        out_shape=[jax.ShapeDtypeStruct((M, n_cols), dt) for dt in out_dtypes] + wb_shapes,
        compiler_params=_params("parallel", "arbitrary"),
        name=name,
    )(x, arr)
    n = len(out_dtypes)
    return outs[:n], ((outs[n], 0, 0) if emit else w)


def _mm_resid_kernel(x_ref, w_ref, r_ref, o_ref, *wb_refs, alpha):
    acc = jnp.dot(x_ref[...], _w_bf16(w_ref, wb_refs[0] if wb_refs else None), preferred_element_type=F32)
    o_ref[...] = r_ref[...] + alpha * acc


def mm_resid(x, w, resid, alpha, *, tm=512, tn=512, name="mm_resid"):
    M, K = x.shape
    N = resid.shape[1]
    tm, tn = _tile(M, tm), _tile(N, tn)
    arr, emit, w_spec, wb_specs, wb_shapes = _w_specs(w, K, N, tn, M, tm)
    outs = pl.pallas_call(
        functools.partial(_mm_resid_kernel, alpha=alpha),
        grid=(M // tm, N // tn),
        in_specs=[pl.BlockSpec((tm, K), lambda i, j: (i, 0)), w_spec,
                  pl.BlockSpec((tm, tn), lambda i, j: (i, j))],
        out_specs=[pl.BlockSpec((tm, tn), lambda i, j: (i, j))] + wb_specs,
        out_shape=[jax.ShapeDtypeStruct((M, N), F32)] + wb_shapes,
        compiler_params=_params("parallel", "arbitrary"),
        name=name,
    )(x, arr, resid)
    return outs[0], ((outs[1], 0, 0) if emit else w)


def _mm_gateup_kernel(x_ref, wg_ref, wu_ref, o_ref, *wb_refs):
    x = x_ref[...]
    g = jnp.dot(x, _w_bf16(wg_ref, wb_refs[0] if wb_refs else None), preferred_element_type=F32)
    u = jnp.dot(x, _w_bf16(wu_ref, wb_refs[1] if wb_refs else None), preferred_element_type=F32)
    o_ref[...] = (_silu(g) * u).astype(o_ref.dtype)


def mm_gateup(x, wg, wu, *, n_cols, tm=2048, tn=256):
    M, K = x.shape
    tm, tn = _tile(M, tm), _tile(n_cols, tn)
    garr, emit, g_spec, gb_specs, gb_shapes = _w_specs(wg, K, n_cols, tn, M, tm)
    uarr, _, u_spec, ub_specs, ub_shapes = _w_specs(wu, K, n_cols, tn, M, tm)
    outs = pl.pallas_call(
        _mm_gateup_kernel,
        grid=(M // tm, n_cols // tn),
        in_specs=[pl.BlockSpec((tm, K), lambda i, j: (i, 0)), g_spec, u_spec],
        out_specs=[pl.BlockSpec((tm, tn), lambda i, j: (i, j))] + gb_specs + ub_specs,
        out_shape=[jax.ShapeDtypeStruct((M, n_cols), BF16)] + gb_shapes + ub_shapes,
        compiler_params=_params("parallel", "arbitrary"),
        name="mm_gateup",
    )(x, garr, uarr)
    if emit:
        return outs[0], (outs[1], 0, 0), (outs[2], 0, 0)
    return outs[0], wg, wu


def _mm_dt_kernel(x_ref, w_ref, b_ref, alog_ref, dt_ref, acum_ref, *wb_refs, L):
    raw = jnp.dot(x_ref[...], _w_bf16(w_ref, wb_refs[0] if wb_refs else None), preferred_element_type=F32)
    dt = _softplus(raw + b_ref[...])
    dt_ref[...] = dt
    acum_ref[...] = _cumsum_rows(dt * (-jnp.exp(alog_ref[...])), L)


def mm_dt(x, w, dt_bias, a_log, *, L, tm=2048):
    M, K = x.shape
    heads = dt_bias.shape[0]
    tm = _tile(M, tm)
    assert tm % L == 0 and L & (L - 1) == 0
    arr, emit, w_spec, wb_specs, wb_shapes = _w_specs(w, K, heads, heads, M, tm)
    outs = pl.pallas_call(
        functools.partial(_mm_dt_kernel, L=L),
        grid=(M // tm, 1),
        in_specs=[pl.BlockSpec((tm, K), lambda i, j: (i, 0)), w_spec,
                  pl.BlockSpec((1, heads), lambda i, j: (0, 0)),
                  pl.BlockSpec((1, heads), lambda i, j: (0, 0))],
        out_specs=[pl.BlockSpec((tm, heads), lambda i, j: (i, 0)),
                   pl.BlockSpec((tm, heads), lambda i, j: (i, 0))] + wb_specs,
        out_shape=[jax.ShapeDtypeStruct((M, heads), F32), jax.ShapeDtypeStruct((M, heads), F32)] + wb_shapes,
        compiler_params=_params("parallel", "arbitrary"),
        name="mm_dt",
    )(x, arr, dt_bias.reshape(1, heads), a_log.reshape(1, heads))
    return outs[0], outs[1], ((outs[2], 0, 0) if emit else w)


def _mm_logf_kernel(x_ref, w_ref, b_ref, o_ref):
    s = jnp.dot(x_ref[...], w_ref[...], preferred_element_type=F32) + b_ref[...]
    o_ref[...] = -_softplus(-s)


def mm_logf(x, w, b):
    M, K = x.shape
    N = w.shape[1]
    tm = _tile(M, 2048)
    return pl.pallas_call(
        _mm_logf_kernel,
        grid=(M // tm,),
        in_specs=[pl.BlockSpec((tm, K), lambda i: (i, 0)),
                  pl.BlockSpec((K, N), lambda i: (0, 0)),
                  pl.BlockSpec((1, N), lambda i: (0, 0))],
        out_specs=pl.BlockSpec((tm, N), lambda i: (i, 0)),
        out_shape=jax.ShapeDtypeStruct((M, N), F32),
        compiler_params=_params("parallel"),
        name="mm_logf",
    )(x, w, b)


def _split3(v):
    hi = v.astype(BF16)
    r1 = v - hi.astype(F32)
    mid = r1.astype(BF16)
    lo = (r1 - mid.astype(F32)).astype(BF16)
    return [hi, mid, lo]


def _ssd_kernel(z_ref, x_ref, b_ref, c_ref, dt_ref, acum_ref,
                cwx_ref, cwb_ref, cwc_ref, cbx_ref, cbb_ref, cbc_ref,
                dx_ref, ng_ref, e_ref,
                h0_ref, sx0_ref, sb0_ref, sc0_ref,
                y_ref, hout_ref, sxo_ref, sbo_ref, sco_ref,
                ht_scr, xp_scr, bp_scr, cp_scr, *, L, GW):
    c = pl.program_id(2)
    P = SSM_HEAD_DIM
    N = LANES
    n_sub = x_ref.shape[1] // GW

    @pl.when(c == 0)
    def _():
        for s in range(n_sub):
            ht_scr[s] = h0_ref[0, s * GW:(s + 1) * GW, :].T
        xp_scr[5:8, :] = sx0_ref[0]
        bp_scr[5:8, :] = sb0_ref[0]
        cp_scr[5:8, :] = sc0_ref[0]

    def conv_silu(v_ref, p_scr, w_ref, bias_ref):
        p_scr[8:8 + L, :] = v_ref[...]
        acc = bias_ref[...] + p_scr[5:5 + L, :] * w_ref[0:1, :]
        for k in range(1, CONV_W):
            acc = acc + p_scr[5 + k:5 + k + L, :] * w_ref[k:k + 1, :]
        p_scr[5:8, :] = p_scr[5 + L:8 + L, :]
        return _silu(acc)

    xs_all = conv_silu(x_ref, xp_scr, cwx_ref, cbx_ref)
    b_all = conv_silu(b_ref, bp_scr, cwb_ref, cbb_ref)
    c_all = conv_silu(c_ref, cp_scr, cwc_ref, cbc_ref)
    sxo_ref[0] = xp_scr[5:8, :]
    sbo_ref[0] = bp_scr[5:8, :]
    sco_ref[0] = cp_scr[5:8, :]

    lhs = jnp.concatenate(_split3(acum_ref[...]) + _split3(dt_ref[...]), axis=0)

    lane = lax.broadcasted_iota(jnp.int32, (L, LANES), 1)
    rowi = lax.broadcasted_iota(jnp.int32, (L, LANES), 0)
    s_idx = jnp.bitwise_and(lane, P - 1)
    r_idx = jnp.right_shift(lane, P.bit_length() - 1)
    eye2 = rowi == s_idx
    tri2 = s_idx <= rowi
    pad = [] if L == P else [jnp.zeros((P - L, LANES), BF16)]

    for s in range(n_sub):
        xs = xs_all[:, s * GW:(s + 1) * GW]
        Bb = b_all[:, s * N:(s + 1) * N].astype(BF16)
        Cb = c_all[:, s * N:(s + 1) * N].astype(BF16)
        ex = jnp.dot(lhs, e_ref[s], preferred_element_type=F32)
        acum_x = ex[0:L] + ex[L:2 * L] + ex[2 * L:3 * L]
        dt_x = ex[3 * L:4 * L] + ex[4 * L:5 * L] + ex[5 * L:6 * L]

        xdt = xs * dt_x
        ht = ht_scr[s]
        y_inter = jnp.dot(Cb, ht.astype(BF16), preferred_element_type=F32) * jnp.exp(acum_x)

        b2 = jnp.concatenate([Bb] + pad + [Bb] + pad, axis=0)
        cb2 = lax.dot_general(Cb, b2, (((1,), (1,)), ((), ())), preferred_element_type=F32)
        y_parts = []
        for j in range(GW // LANES):
            sl = slice(j * LANES, (j + 1) * LANES)
            ax = acum_x[:, sl]
            v = jnp.sum(jnp.where(eye2, ax, 0.0), axis=0, keepdims=True)
            decay = jnp.exp(jnp.where(tri2, ax - v, -jnp.inf))
            m2 = (cb2 * decay).astype(BF16)
            xp = xdt[:, sl]
            top = jnp.where(r_idx == 0, xp, 0.0).astype(BF16)
            bot = jnp.where(r_idx == 1, xp, 0.0).astype(BF16)
            bd = jnp.concatenate([top] + pad + [bot] + pad, axis=0)
            y_parts.append(jnp.dot(m2, bd, preferred_element_type=F32))
        y = jnp.concatenate(y_parts, axis=1) + y_inter + dx_ref[:, s * GW:(s + 1) * GW] * xs

        y = y * _silu(z_ref[:, s * GW:(s + 1) * GW])
        y = y * lax.rsqrt(jnp.mean(y * y, axis=-1, keepdims=True) + NORM_EPS)
        y_ref[:, s * GW:(s + 1) * GW] = (y * ng_ref[:, s * GW:(s + 1) * GW]).astype(y_ref.dtype)

        last = acum_x[L - 1:L, :]
        xt = (xdt * jnp.exp(last - acum_x)).astype(BF16)
        upd = lax.dot_general(Bb, xt, (((0,), (0,)), ((), ())), preferred_element_type=F32)
        ht_new = ht * jnp.exp(last) + upd
        ht_scr[s] = ht_new

        @pl.when(c == pl.num_programs(2) - 1)
        def _():
            hout_ref[0, s * GW:(s + 1) * GW, :] = ht_new.T


def ssd_mixer(proj, dt, acum, conv_state, ssm_state, conv_w, conv_b, d_skip, norm_g,
              *, Bsz, T, L, d_inner, groups, n_state):
    M = Bsz * T
    heads = dt.shape[1]
    P = SSM_HEAD_DIM
    GW = d_inner // groups
    N = n_state
    R = GW // P
    nc = T // L
    ns = SSD_GROUPS_PER_STEP
    assert N == LANES and heads == LANES and GW % LANES == 0 and L % 16 == 0 and L <= P and groups % ns == 0
    SW, SN = ns * GW, ns * N
    xb, bb, cb = d_inner // SW, 2 * d_inner // SN, (2 * d_inner + groups * N) // SN
    kb, kc = d_inner // SN, (d_inner + groups * N) // SN

    expand = np.zeros((groups, heads, GW), np.float32)
    for g in range(groups):
        for r in range(R):
            expand[g, g * R + r, r * P:(r + 1) * P] = 1.0
    expand = jnp.asarray(expand, BF16)

    cbias = conv_b.reshape(1, -1)
    d_x = jnp.repeat(d_skip, P).reshape(1, d_inner)
    h0 = ssm_state.reshape(Bsz, heads * P, N)

    row = lambda b, g, c: b * nc + c
    in_specs = [
        pl.BlockSpec((L, SW), lambda b, g, c: (row(b, g, c), g)),
        pl.BlockSpec((L, SW), lambda b, g, c: (row(b, g, c), xb + g)),
        pl.BlockSpec((L, SN), lambda b, g, c: (row(b, g, c), bb + g)),
        pl.BlockSpec((L, SN), lambda b, g, c: (row(b, g, c), cb + g)),
        pl.BlockSpec((L, heads), lambda b, g, c: (row(b, g, c), 0)),
        pl.BlockSpec((L, heads), lambda b, g, c: (row(b, g, c), 0)),
        pl.BlockSpec((CONV_W, SW), lambda b, g, c: (0, g)),
        pl.BlockSpec((CONV_W, SN), lambda b, g, c: (0, kb + g)),
        pl.BlockSpec((CONV_W, SN), lambda b, g, c: (0, kc + g)),
        pl.BlockSpec((1, SW), lambda b, g, c: (0, g)),
        pl.BlockSpec((1, SN), lambda b, g, c: (0, kb + g)),
        pl.BlockSpec((1, SN), lambda b, g, c: (0, kc + g)),
        pl.BlockSpec((1, SW), lambda b, g, c: (0, g)),
        pl.BlockSpec((1, SW), lambda b, g, c: (0, g)),
        pl.BlockSpec((ns, heads, GW), lambda b, g, c: (g, 0, 0)),
        pl.BlockSpec((1, SW, N), lambda b, g, c: (b, g, 0)),
        pl.BlockSpec((1, CONV_W - 1, SW), lambda b, g, c: (b, 0, g)),
        pl.BlockSpec((1, CONV_W - 1, SN), lambda b, g, c: (b, 0, kb + g)),
        pl.BlockSpec((1, CONV_W - 1, SN), lambda b, g, c: (b, 0, kc + g)),
    ]
    out_specs = [
        pl.BlockSpec((L, SW), lambda b, g, c: (row(b, g, c), g)),
        pl.BlockSpec((1, SW, N), lambda b, g, c: (b, g, 0)),
        pl.BlockSpec((1, CONV_W - 1, SW), lambda b, g, c: (b, 0, g)),
        pl.BlockSpec((1, CONV_W - 1, SN), lambda b, g, c: (b, 0, g)),
        pl.BlockSpec((1, CONV_W - 1, SN), lambda b, g, c: (b, 0, g)),
    ]
    out_shape = [
        jax.ShapeDtypeStruct((M, d_inner), BF16),
        jax.ShapeDtypeStruct((Bsz, heads * P, N), F32),
        jax.ShapeDtypeStruct((Bsz, CONV_W - 1, d_inner), F32),
        jax.ShapeDtypeStruct((Bsz, CONV_W - 1, groups * N), F32),
        jax.ShapeDtypeStruct((Bsz, CONV_W - 1, groups * N), F32),
    ]
    y, h_new, sx, sb, sc = pl.pallas_call(
        functools.partial(_ssd_kernel, L=L, GW=GW),
        grid=(Bsz, groups // ns, nc),
        in_specs=in_specs,
        out_specs=out_specs,
        out_shape=out_shape,
        scratch_shapes=[pltpu.VMEM((ns, N, GW), F32),
                        pltpu.VMEM((8 + L, SW), F32),
                        pltpu.VMEM((8 + L, SN), F32),
                        pltpu.VMEM((8 + L, SN), F32)],
        compiler_params=_params("parallel", "parallel", "arbitrary"),
        name="ssd_mixer",
    )(proj, proj, proj, proj, dt, acum, conv_w, conv_w, conv_w, cbias, cbias, cbias,
      d_x, norm_g.reshape(1, d_inner), expand, h0, conv_state, conv_state, conv_state)
    new_conv = jnp.concatenate([sx, sb, sc], axis=-1)
    return y, new_conv, h_new.reshape(Bsz, heads, P, N)


def _fcum_kernel(past_ref, new_ref, o_ref, w_scr, *, past, Tn, heads):
    w_scr[...] = jnp.zeros_like(w_scr)
    if past:
        w_scr[0:past, 0:heads] = past_ref[0]
    w_scr[past:past + Tn, :] = new_ref[...]
    n = w_scr.shape[0]
    o_ref[0] = _cumsum_rows(w_scr[...], 1 << (n - 1).bit_length()).T


def fcum(cache_logf, logf_new, *, Bsz, Tn):
    past, heads = cache_logf.shape[1], cache_logf.shape[2]
    t_pad = -(-(past + Tn) // LANES) * LANES
    if past == 0:
        cache_logf = jnp.zeros((Bsz, 8, heads), F32)
    return pl.pallas_call(
        functools.partial(_fcum_kernel, past=past, Tn=Tn, heads=heads),
        grid=(Bsz,),
        in_specs=[pl.BlockSpec((1,) + cache_logf.shape[1:], lambda b: (b, 0, 0)),
                  pl.BlockSpec((Tn, LANES), lambda b: (b, 0))],
        out_specs=pl.BlockSpec((1, LANES, t_pad), lambda b: (b, 0, 0)),
        out_shape=jax.ShapeDtypeStruct((Bsz, LANES, t_pad), F32),
        scratch_shapes=[pltpu.VMEM((t_pad, LANES), F32)],
        compiler_params=_params("parallel"),
        name="fcum",
    )(cache_logf, logf_new)


def _row_to_cols(frow):
    return jnp.broadcast_to(frow, (LANES, frow.shape[1])).T


def _attn_prompt_kernel(q_ref, k_ref, v_ref, f_ref, o_ref, vt_scr, fk_scr, *, tb, hd, scale):
    T = q_ref.shape[0]
    nh = q_ref.shape[1] // hd
    nb = T // tb
    h0 = (pl.program_id(1) % (8 // nh)) * nh
    frows = [f_ref[0, pl.ds(h0 + i, 1), :] for i in range(nh)]
    for i in range(nh):
        fk_scr[i] = _row_to_cols(frows[i])
        for j in range(nb):
            vt_scr[i, j] = v_ref[j * tb:(j + 1) * tb, i * hd:(i + 1) * hd].astype(F32).T.astype(BF16)
    kofs = lax.broadcasted_iota(jnp.int32, (tb, tb), 0)
    qofs = lax.broadcasted_iota(jnp.int32, (tb, tb), 1)

    for qi in range(nb):
        qs = slice(qi * tb, (qi + 1) * tb)
        q_t = [q_ref[qs, i * hd:(i + 1) * hd].astype(F32).T.astype(BF16) for i in range(nh)]

        def block(j, carry, masked):
            start = j * tb if isinstance(j, int) else pl.multiple_of(j * tb, tb)
            qk = [jnp.dot(k_ref[pl.ds(start, tb), i * hd:(i + 1) * hd], q_t[i], preferred_element_type=F32)
                  for i in range(nh)]
            out = []
            for i in range(nh):
                m, l, acc = carry[3 * i:3 * i + 3]
                fk = fk_scr[i, pl.ds(start, tb), :]
                s = qk[i] * scale
                s = s + frows[i][:, qs] - jnp.concatenate([fk] * (tb // LANES), axis=1)
                if masked:
                    s = jnp.where(kofs <= qofs, s, -jnp.inf)
                m_new = jnp.maximum(m, jnp.max(s, axis=0, keepdims=True))
                alpha = jnp.exp(m - m_new)
                p = jnp.exp(s - m_new)
                l = alpha * l + jnp.sum(p, axis=0, keepdims=True)
                acc = alpha * acc + jnp.dot(vt_scr[i, j], p.astype(BF16), preferred_element_type=F32)
                out += [m_new, l, acc]
            return tuple(out)

        carry = (jnp.full((1, tb), -jnp.inf, F32), jnp.zeros((1, tb), F32), jnp.zeros((hd, tb), F32)) * nh
        if qi:
            carry = lax.fori_loop(0, qi, functools.partial(block, masked=False), carry)
        carry = block(qi, carry, True)
        for i in range(nh):
            o_ref[qs, i * hd:(i + 1) * hd] = (carry[3 * i + 2] / carry[3 * i + 1]).T.astype(o_ref.dtype)


def attn_prompt(q, k, v, f_t, *, Bsz, T, heads, hd):
    tb = _tile(T, ATTN_BLOCK)
    nh = ATTN_HEADS_PER_STEP
    assert tb % LANES == 0 and heads % 8 == 0 and 8 % nh == 0 and hd == LANES
    blk = pl.BlockSpec((T, nh * hd), lambda b, h: (b, h))
    return pl.pallas_call(
        functools.partial(_attn_prompt_kernel, tb=tb, hd=hd, scale=hd ** -0.5),
        grid=(Bsz, heads // nh),
        in_specs=[blk, blk, blk, pl.BlockSpec((1, 8, T), lambda b, h: (b, h * nh // 8, 0))],
        out_specs=blk,
        out_shape=jax.ShapeDtypeStruct((Bsz * T, heads * hd), BF16),
        scratch_shapes=[pltpu.VMEM((nh, T // tb, hd, tb), BF16), pltpu.VMEM((nh, T, LANES), F32)],
        compiler_params=_params("parallel", "parallel"),
        name="attn_prompt",
    )(q, k, v, f_t)


def _attn_sample_kernel(q_ref, kc_ref, vc_ref, kn_ref, vn_ref, f_ref, o_ref, *, Tn, past, hd, nh, scale):
    zrows = jnp.zeros((LANES - Tn, hd), BF16)
    rowi = lax.broadcasted_iota(jnp.int32, (Tn, LANES), 0)
    coli = lax.broadcasted_iota(jnp.int32, (Tn, LANES), 1)
    nt = (((1,), (1,)), ((), ()))
    for hh in range(nh):
        sl = slice(hh * hd, (hh + 1) * hd)
        q = q_ref[:, sl]
        kc = kc_ref[0, :, sl].astype(BF16)
        vc = vc_ref[0, :, sl].astype(BF16)
        kn = jnp.concatenate([kn_ref[:, sl], zrows], axis=0)
        vn = jnp.concatenate([vn_ref[:, sl], zrows], axis=0)
        f_past = f_ref[0, hh:hh + 1, 0:past]
        f_new = f_ref[0, hh:hh + 1, past:past + LANES]
        fq_col = _row_to_cols(f_new)[0:Tn, 0:1]
        s1 = lax.dot_general(q, kc, nt, preferred_element_type=F32) * scale + fq_col - f_past
        s2 = lax.dot_general(q, kn, nt, preferred_element_type=F32) * scale + fq_col - f_new
        s2 = jnp.where(coli <= rowi, s2, -jnp.inf)
        m = jnp.maximum(jnp.max(s1, axis=1, keepdims=True), jnp.max(s2, axis=1, keepdims=True))
        p1 = jnp.exp(s1 - m)
        p2 = jnp.exp(s2 - m)
        l = jnp.sum(p1, axis=1, keepdims=True) + jnp.sum(p2, axis=1, keepdims=True)
        o = jnp.dot(p1.astype(BF16), vc, preferred_element_type=F32)
        o = o + jnp.dot(p2.astype(BF16), vn, preferred_element_type=F32)
        o_ref[:, sl] = (o / l).astype(o_ref.dtype)


def attn_sample(q, cache_k, cache_v, k_new, v_new, f_t, *, Bsz, Tn, heads, hd):
    past = cache_k.shape[1]
    nh = 8
    assert past % LANES == 0 and Tn % 16 == 0 and Tn <= LANES and heads % nh == 0
    ck = cache_k.reshape(Bsz, past, heads * hd)
    cv = cache_v.reshape(Bsz, past, heads * hd)
    return pl.pallas_call(
        functools.partial(_attn_sample_kernel, Tn=Tn, past=past, hd=hd, nh=nh, scale=hd ** -0.5),
        grid=(Bsz, heads // nh),
        in_specs=[pl.BlockSpec((Tn, nh * hd), lambda b, g: (b, g)),
                  pl.BlockSpec((1, past, nh * hd), lambda b, g: (b, 0, g)),
                  pl.BlockSpec((1, past, nh * hd), lambda b, g: (b, 0, g)),
                  pl.BlockSpec((Tn, nh * hd), lambda b, g: (b, g)),
                  pl.BlockSpec((Tn, nh * hd), lambda b, g: (b, g)),
                  pl.BlockSpec((1, nh, f_t.shape[2]), lambda b, g: (b, g, 0))],
        out_specs=pl.BlockSpec((Tn, nh * hd), lambda b, g: (b, g)),
        out_shape=jax.ShapeDtypeStruct((Bsz * Tn, heads * hd), BF16),
        compiler_params=_params("parallel", "parallel"),
        name="attn_sample",
    )(q, ck, cv, k_new, v_new, f_t)


def _trunk(x, conv_st, ssm_st, cache_k, cache_v, cache_logf, w, p):
    Bsz, T, D = x.shape
    M = Bsz * T
    heads, hd = cache_k.shape[2], cache_k.shape[3]
    d_inner = p["a_norm"].shape[-1]
    d_ff = p["d_ff"]
    n_state = ssm_st.shape[-1]
    groups = (conv_st.shape[-1] - d_inner) // (2 * n_state)
    d_zxbc = 2 * d_inner + 2 * groups * n_state
    L = SSD_CHUNK if T % SSD_CHUNK == 0 else T
    small = M <= 512
    wb = {}

    def ffn(h, idx):
        xn = rmsnorm(h, p["norm_ffn"].reshape(-1, D)[idx], BF16)
        a, wb[f"wg{idx}"], wb[f"wu{idx}"] = mm_gateup(xn, w[f"wg{idx}"], w[f"wu{idx}"], n_cols=d_ff)
        h, wb[f"wd{idx}"] = mm_resid(a, w[f"wd{idx}"], h, 0.5, tn=256 if small else 512, name="mm_down")
        return h

    h = x.reshape(M, D)
    h = ffn(h, 0)
    u = rmsnorm(h, p["norm_mix"][0], BF16)
    (proj,), wb["w_in"] = mm_plain(u, w["w_in"], [F32], n_cols=d_zxbc, tn=512 if small else 1024,
                                   name="mm_in_proj")
    dt, acum, wb["w_dt"] = mm_dt(u, w["w_dt"], p["a_dt_bias"][0], p["a_A_log"][0], L=L)
    y, new_conv, new_ssm = ssd_mixer(
        proj, dt, acum, conv_st[0], ssm_st[0], p["a_conv_w"][0], p["a_conv_b"][0], p["a_D"][0], p["a_norm"][0],
        Bsz=Bsz, T=T, L=L, d_inner=d_inner, groups=groups, n_state=n_state)
    h, wb["w_out"] = mm_resid(y, w["w_out"], h, 1.0, tn=256 if small else 512, name="mm_out_proj")
    h = ffn(h, 1)
    xn = rmsnorm(h, p["kv_norm"], BF16)
    tn = 512 if small else 1024
    (k32, kb), wb["w_k"] = mm_plain(xn, w["w_k"], [F32, BF16], n_cols=D, tn=tn, name="mm_k")
    (v32, vb), wb["w_v"] = mm_plain(xn, w["w_v"], [F32, BF16], n_cols=D, tn=tn, name="mm_v")
    logf = mm_logf(xn, p["w_f"], p["b_f"])
    f_t = fcum(cache_logf, logf, Bsz=Bsz, Tn=T)
    h = ffn(h, 2)
    u = rmsnorm(h, p["norm_mix"][1], BF16)
    (q,), wb["w_q"] = mm_plain(u, w["w_q"], [BF16], n_cols=D, tn=tn, name="mm_q")
    if cache_k.shape[1] == 0:
        o = attn_prompt(q, kb, vb, f_t, Bsz=Bsz, T=T, heads=heads, hd=hd)
    else:
        o = attn_sample(q, cache_k, cache_v, kb, vb, f_t, Bsz=Bsz, Tn=T, heads=heads, hd=hd)
    h, wb["w_o"] = mm_resid(o, w["w_o"], h, 1.0, name="mm_wo")
    h = ffn(h, 3)
    y_out = rmsnorm(h, p["norm_final"], F32).reshape(Bsz, T, D)
    outs = (y_out, new_conv[None], new_ssm[None],
            k32.reshape(Bsz, T, heads, hd), v32.reshape(Bsz, T, heads, hd),
            logf[:, :heads].reshape(Bsz, T, heads))
    return outs, wb


def kernel(x_prompt, x_sample, state_conv, state_ssm, cache_k, cache_v, cache_logf, norm_ffn, ffn_wg, ffn_wu,
           ffn_wd, norm_mix, a_w_in, a_conv_w, a_conv_b, a_dt_bias, a_A_log, a_D, a_norm, a_w_out, kv_norm,
           w_kvf, b_f, b_wq, b_wo, norm_final):
    assert norm_mix.shape[0] == 2 and a_w_in.shape[0] == 1 and b_wq.shape[0] == 1
    D = x_prompt.shape[-1]
    F = ffn_wg.shape[-1]
    heads = cache_k.shape[2]
    d_zxbc = a_w_in.shape[-1] - a_dt_bias.shape[-1]
    p = {"norm_ffn": norm_ffn, "norm_mix": norm_mix, "a_conv_w": a_conv_w, "a_conv_b": a_conv_b,
         "a_dt_bias": a_dt_bias, "a_A_log": a_A_log, "a_D": a_D, "a_norm": a_norm, "kv_norm": kv_norm,
         "norm_final": norm_final, "d_ff": F,
         "w_f": jnp.pad(w_kvf[:, 2 * D:], ((0, 0), (0, LANES - heads))).astype(BF16),
         "b_f": jnp.pad(b_f, (0, LANES - heads)).reshape(1, LANES)}
    wg, wu, wd = ffn_wg.reshape(-1, F), ffn_wu.reshape(-1, F), ffn_wd.reshape(-1, D)
    w = {"w_in": (a_w_in[0], 0, 0), "w_dt": (a_w_in[0], 0, d_zxbc), "w_out": (a_w_out[0], 0, 0),
         "w_k": (w_kvf, 0, 0), "w_v": (w_kvf, 0, D), "w_q": (b_wq[0], 0, 0), "w_o": (b_wo[0], 0, 0)}
    for idx in range(wg.shape[0] // D):
        w[f"wg{idx}"], w[f"wu{idx}"], w[f"wd{idx}"] = (wg, idx, 0), (wu, idx, 0), (wd, idx, 0)

    bp = x_prompt.shape[0]
    zero_conv = jnp.zeros((state_conv.shape[0], bp) + state_conv.shape[2:], state_conv.dtype)
    zero_ssm = jnp.zeros((state_ssm.shape[0], bp) + state_ssm.shape[2:], state_ssm.dtype)
    empty_kv = jnp.zeros((bp, 0) + cache_k.shape[2:], cache_k.dtype)
    empty_lf = jnp.zeros((bp, 0, heads), cache_logf.dtype)
    (y_s, s_conv, s_ssm, s_k, s_v, s_logf), wb = _trunk(
        x_sample, state_conv, state_ssm, cache_k, cache_v, cache_logf, w, p)
    (y_p, p_conv, p_ssm, p_k, p_v, p_logf), _ = _trunk(
        x_prompt, zero_conv, zero_ssm, empty_kv, empty_kv, empty_lf, wb, p)
    return (y_p, y_s, p_conv, p_ssm, p_k, p_v, p_logf, s_conv, s_ssm, s_k, s_v, s_logf)
```

```python
import functools

import numpy as np
import jax
import jax.numpy as jnp
from jax import lax
from jax.experimental import pallas as pl
from jax.experimental.pallas import tpu as pltpu

NORM_EPS = 1e-5
SSD_CHUNK = 64
SSM_HEAD_DIM = 64
CONV_W = 4
LANES = 128
LOG2E = 1.4426950408889634
VMEM_LIMIT_BYTES = 56 * 1024 * 1024
SSD_GROUPS_PER_STEP = 2
ATTN_BLOCK = 256
ATTN_HEADS_PER_STEP = 4
BF16 = jnp.bfloat16
F32 = jnp.float32


def _params(*sem):
    return pltpu.CompilerParams(dimension_semantics=sem, vmem_limit_bytes=VMEM_LIMIT_BYTES)


def _tile(n, pref):
    if n <= pref:
        return n
    t = 1 << (pref.bit_length() - 1)
    while n % t:
        t //= 2
    return t


def _sigmoid(x):
    return 1.0 / (1.0 + jnp.exp(-x))


def _silu(x):
    return x * _sigmoid(x)


def _softplus(x):
    return jnp.maximum(x, 0.0) + jnp.log1p(jnp.exp(-jnp.abs(x)))


def _cumsum_rows(x, seg):
    rowi = jnp.bitwise_and(lax.broadcasted_iota(jnp.int32, x.shape, 0), seg - 1)
    s = 1
    while s < seg:
        x = x + jnp.where(rowi >= s, pltpu.roll(x, s, 0), 0.0)
        s *= 2
    return x


def _rmsnorm_kernel(x_ref, g_ref, o_ref):
    x = x_ref[...]
    y = x * lax.rsqrt(jnp.mean(x * x, axis=-1, keepdims=True) + NORM_EPS)
    o_ref[...] = (y * g_ref[...]).astype(o_ref.dtype)


def rmsnorm(x, g, out_dtype):
    M, D = x.shape
    tm = _tile(M, 512)
    return pl.pallas_call(
        _rmsnorm_kernel,
        grid=(M // tm,),
        in_specs=[pl.BlockSpec((tm, D), lambda i: (i, 0)),
                  pl.BlockSpec((1, D), lambda i: (0, 0))],
        out_specs=pl.BlockSpec((tm, D), lambda i: (i, 0)),
        out_shape=jax.ShapeDtypeStruct((M, D), out_dtype),
        compiler_params=_params("parallel"),
        name="rmsnorm",
    )(x, g.reshape(1, D))


def _fold_lanes(v):
    parts = [v[:, k * LANES:(k + 1) * LANES] for k in range(v.shape[1] // LANES)]
    while len(parts) > 1:
        parts = [a + b for a, b in zip(parts[::2], parts[1::2])] + parts[len(parts) & ~1:]
    return parts[0]


def _row_scale(ssq_ref, d):
    return lax.rsqrt(jnp.sum(ssq_ref[...], axis=-1, keepdims=True) * (1.0 / d) + NORM_EPS)


def _norm_prep_kernel(x_ref, g_ref, hb_ref, ssq_ref):
    x = x_ref[...]
    hb_ref[...] = (x * g_ref[...]).astype(hb_ref.dtype)
    ssq_ref[...] = _fold_lanes(x * x)


def norm_prep(x, g):
    M, D = x.shape
    tm = _tile(M, 512)
    hb, ssq = pl.pallas_call(
        _norm_prep_kernel,
        grid=(M // tm,),
        in_specs=[pl.BlockSpec((tm, D), lambda i: (i, 0)),
                  pl.BlockSpec((1, D), lambda i: (0, 0))],
        out_specs=[pl.BlockSpec((tm, D), lambda i: (i, 0)),
                   pl.BlockSpec((tm, LANES), lambda i: (i, 0))],
        out_shape=[jax.ShapeDtypeStruct((M, D), BF16), jax.ShapeDtypeStruct((M, LANES), F32)],
        compiler_params=_params("parallel"),
        name="norm_prep",
    )(x, g.reshape(1, D))
    return hb, ssq


def _w_bf16(w_ref, wb_ref):
    w = w_ref[...]
    if wb_ref is not None:
        w = w.astype(BF16)
        wb_ref[...] = w
    return w


def _w_specs(w, K, n_cols, tn, M, tm):
    arr, row_blk, col_off = w
    emit = arr.dtype != BF16
    assert col_off % tn == 0 and n_cols % tn == 0
    assert not emit or M == tm
    cb = col_off // tn
    in_spec = pl.BlockSpec((K, tn), lambda i, j: (row_blk, cb + j))
    out_specs = [pl.BlockSpec((K, tn), lambda i, j: (0, j))] if emit else []
    out_shapes = [jax.ShapeDtypeStruct((K, n_cols), BF16)] if emit else []
    return arr, emit, in_spec, out_specs, out_shapes


def _x_specs(xn, tm):
    hb, ssq = xn
    return [pl.BlockSpec((tm, hb.shape[1]), lambda i, j: (i, 0)),
            pl.BlockSpec((tm, LANES), lambda i, j: (i, 0))]


def _mm_plain_kernel(x_ref, ssq_ref, w_ref, *o_refs, emit):
    wb_ref = o_refs[-1] if emit else None
    acc = jnp.dot(x_ref[...], _w_bf16(w_ref, wb_ref), preferred_element_type=F32)
    acc = acc * _row_scale(ssq_ref, x_ref.shape[1])
    for o_ref in o_refs[:len(o_refs) - emit]:
        o_ref[...] = acc.astype(o_ref.dtype)


def mm_plain(xn, w, out_dtypes, *, n_cols, tm=1024, tn=1024, name="mm_plain"):
    M, K = xn[0].shape
    tm, tn = _tile(M, tm), _tile(n_cols, tn)
    arr, emit, w_spec, wb_specs, wb_shapes = _w_specs(w, K, n_cols, tn, M, tm)
    outs = pl.pallas_call(
        functools.partial(_mm_plain_kernel, emit=emit),
        grid=(M // tm, n_cols // tn),
        in_specs=_x_specs(xn, tm) + [w_spec],
        out_specs=[pl.BlockSpec((tm, tn), lambda i, j: (i, j)) for _ in out_dtypes] + wb_specs,
        out_shape=[jax.ShapeDtypeStruct((M, n_cols), dt) for dt in out_dtypes] + wb_shapes,
        compiler_params=_params("parallel", "arbitrary"),
        name=name,
    )(*xn, arr)
    n = len(out_dtypes)
    return outs[:n], ((outs[n], 0, 0) if emit else w)


def _mm_resid_kernel(*refs, alpha, n_norm, emit):
    x_ref, w_ref, r_ref = refs[:3]
    g_refs = refs[3:3 + n_norm]
    o_ref = refs[3 + n_norm]
    hb_refs = refs[4 + n_norm:4 + 2 * n_norm]
    ssq_ref = refs[4 + 2 * n_norm] if n_norm else None
    wb_ref = refs[-1] if emit else None
    acc = jnp.dot(x_ref[...], _w_bf16(w_ref, wb_ref), preferred_element_type=F32)
    h = r_ref[...] + alpha * acc
    o_ref[...] = h
    for g_ref, hb_ref in zip(g_refs, hb_refs):
        hb_ref[...] = (h * g_ref[...]).astype(hb_ref.dtype)
    if n_norm:
        part = _fold_lanes(h * h)
        first = pl.program_id(1) == 0

        @pl.when(first)
        def _():
            ssq_ref[...] = part

        @pl.when(jnp.logical_not(first))
        def _():
            ssq_ref[...] += part


def mm_resid(x, w, resid, alpha, gains=(), *, tm=512, tn=512, name="mm_resid"):
    M, K = x.shape
    N = resid.shape[1]
    tm, tn = _tile(M, tm), _tile(N, tn)
    n_norm = len(gains)
    arr, emit, w_spec, wb_specs, wb_shapes = _w_specs(w, K, N, tn, M, tm)
    tile = pl.BlockSpec((tm, tn), lambda i, j: (i, j))
    outs = pl.pallas_call(
        functools.partial(_mm_resid_kernel, alpha=alpha, n_norm=n_norm, emit=emit),
        grid=(M // tm, N // tn),
        in_specs=[pl.BlockSpec((tm, K), lambda i, j: (i, 0)), w_spec, tile]
                 + [pl.BlockSpec((1, tn), lambda i, j: (0, j))] * n_norm,
        out_specs=[tile] * (1 + n_norm) + [pl.BlockSpec((tm, LANES), lambda i, j: (i, 0))] * bool(n_norm) + wb_specs,
        out_shape=[jax.ShapeDtypeStruct((M, N), F32)] + [jax.ShapeDtypeStruct((M, N), BF16)] * n_norm
                  + [jax.ShapeDtypeStruct((M, LANES), F32)] * bool(n_norm) + wb_shapes,
        compiler_params=_params("parallel", "arbitrary"),
        name=name,
    )(x, arr, resid, *[g.reshape(1, N) for g in gains])
    xns = [(outs[1 + k], outs[1 + n_norm]) for k in range(n_norm)]
    return outs[0], xns, ((outs[-1], 0, 0) if emit else w)


def _mm_gateup_kernel(x_ref, ssq_ref, wg_ref, wu_ref, o_ref, *wb_refs):
    x = x_ref[...]
    r = _row_scale(ssq_ref, x.shape[1])
    g = jnp.dot(x, _w_bf16(wg_ref, wb_refs[0] if wb_refs else None), preferred_element_type=F32) * r
    u = jnp.dot(x, _w_bf16(wu_ref, wb_refs[1] if wb_refs else None), preferred_element_type=F32) * r
    o_ref[...] = (_silu(g) * u).astype(o_ref.dtype)


def mm_gateup(xn, wg, wu, *, n_cols, tm=2048, tn=256):
    M, K = xn[0].shape
    tm, tn = _tile(M, tm), _tile(n_cols, tn)
    garr, emit, g_spec, gb_specs, gb_shapes = _w_specs(wg, K, n_cols, tn, M, tm)
    uarr, _, u_spec, ub_specs, ub_shapes = _w_specs(wu, K, n_cols, tn, M, tm)
    outs = pl.pallas_call(
        _mm_gateup_kernel,
        grid=(M // tm, n_cols // tn),
        in_specs=_x_specs(xn, tm) + [g_spec, u_spec],
        out_specs=[pl.BlockSpec((tm, tn), lambda i, j: (i, j))] + gb_specs + ub_specs,
        out_shape=[jax.ShapeDtypeStruct((M, n_cols), BF16)] + gb_shapes + ub_shapes,
        compiler_params=_params("parallel", "arbitrary"),
        name="mm_gateup",
    )(*xn, garr, uarr)
    if emit:
        return outs[0], (outs[1], 0, 0), (outs[2], 0, 0)
    return outs[0], wg, wu


def _mm_dt_kernel(x_ref, ssq_ref, w_ref, b_ref, alog_ref, dt_ref, acum_ref, *wb_refs, L):
    raw = jnp.dot(x_ref[...], _w_bf16(w_ref, wb_refs[0] if wb_refs else None), preferred_element_type=F32)
    dt = _softplus(raw * _row_scale(ssq_ref, x_ref.shape[1]) + b_ref[...])
    dt_ref[...] = dt
    acum_ref[...] = _cumsum_rows(dt * (-jnp.exp(alog_ref[...])), L)


def mm_dt(xn, w, dt_bias, a_log, *, L, tm=2048):
    M, K = xn[0].shape
    heads = dt_bias.shape[0]
    tm = _tile(M, tm)
    assert tm % L == 0 and L & (L - 1) == 0
    arr, emit, w_spec, wb_specs, wb_shapes = _w_specs(w, K, heads, heads, M, tm)
    vec = pl.BlockSpec((1, heads), lambda i, j: (0, 0))
    outs = pl.pallas_call(
        functools.partial(_mm_dt_kernel, L=L),
        grid=(M // tm, 1),
        in_specs=_x_specs(xn, tm) + [w_spec, vec, vec],
        out_specs=[pl.BlockSpec((tm, heads), lambda i, j: (i, 0))] * 2 + wb_specs,
        out_shape=[jax.ShapeDtypeStruct((M, heads), F32)] * 2 + wb_shapes,
        compiler_params=_params("parallel", "arbitrary"),
        name="mm_dt",
    )(*xn, arr, dt_bias.reshape(1, heads), a_log.reshape(1, heads))
    return outs[0], outs[1], ((outs[2], 0, 0) if emit else w)


def _mm_logf_kernel(x_ref, ssq_ref, w_ref, b_ref, o_ref):
    s = jnp.dot(x_ref[...], w_ref[...], preferred_element_type=F32)
    s = s * _row_scale(ssq_ref, x_ref.shape[1]) + b_ref[...]
    o_ref[...] = -_softplus(-s)


def mm_logf(xn, w, b):
    M, K = xn[0].shape
    N = w.shape[1]
    tm = _tile(M, 2048)
    return pl.pallas_call(
        _mm_logf_kernel,
        grid=(M // tm, 1),
        in_specs=_x_specs(xn, tm) + [pl.BlockSpec((K, N), lambda i, j: (0, 0)),
                                     pl.BlockSpec((1, N), lambda i, j: (0, 0))],
        out_specs=pl.BlockSpec((tm, N), lambda i, j: (i, 0)),
        out_shape=jax.ShapeDtypeStruct((M, N), F32),
        compiler_params=_params("parallel", "arbitrary"),
        name="mm_logf",
    )(*xn, w, b)


def _split3(v):
    hi = v.astype(BF16)
    r1 = v - hi.astype(F32)
    mid = r1.astype(BF16)
    lo = (r1 - mid.astype(F32)).astype(BF16)
    return [hi, mid, lo]


def _ssd_kernel(z_ref, x_ref, b_ref, c_ref, dt_ref, acum_ref,
                cwx_ref, cwb_ref, cwc_ref, cbx_ref, cbb_ref, cbc_ref,
                dx_ref, ng_ref, e_ref,
                h0_ref, sx0_ref, sb0_ref, sc0_ref,
                y_ref, hout_ref, sxo_ref, sbo_ref, sco_ref,
                ht_scr, xp_scr, bp_scr, cp_scr, *, L, GW):
    c = pl.program_id(2)
    P = SSM_HEAD_DIM
    N = LANES
    n_sub = x_ref.shape[1] // GW

    @pl.when(c == 0)
    def _():
        for s in range(n_sub):
            ht_scr[s] = h0_ref[0, s * GW:(s + 1) * GW, :].T
        xp_scr[5:8, :] = sx0_ref[0]
        bp_scr[5:8, :] = sb0_ref[0]
        cp_scr[5:8, :] = sc0_ref[0]

    def conv_silu(v_ref, p_scr, w_ref, bias_ref):
        p_scr[8:8 + L, :] = v_ref[...]
        acc = bias_ref[...] + p_scr[5:5 + L, :] * w_ref[0:1, :]
        for k in range(1, CONV_W):
            acc = acc + p_scr[5 + k:5 + k + L, :] * w_ref[k:k + 1, :]
        p_scr[5:8, :] = p_scr[5 + L:8 + L, :]
        return _silu(acc)

    xs_all = conv_silu(x_ref, xp_scr, cwx_ref, cbx_ref)
    b_all = conv_silu(b_ref, bp_scr, cwb_ref, cbb_ref)
    c_all = conv_silu(c_ref, cp_scr, cwc_ref, cbc_ref)
    sxo_ref[0] = xp_scr[5:8, :]
    sbo_ref[0] = bp_scr[5:8, :]
    sco_ref[0] = cp_scr[5:8, :]

    lhs = jnp.concatenate(_split3(acum_ref[...]) + _split3(dt_ref[...]), axis=0)

    lane = lax.broadcasted_iota(jnp.int32, (L, LANES), 1)
    rowi = lax.broadcasted_iota(jnp.int32, (L, LANES), 0)
    s_idx = jnp.bitwise_and(lane, P - 1)
    r_idx = jnp.right_shift(lane, P.bit_length() - 1)
    eye2 = rowi == s_idx
    tri2 = s_idx <= rowi
    pad = [] if L == P else [jnp.zeros((P - L, LANES), BF16)]

    for s in range(n_sub):
        xs = xs_all[:, s * GW:(s + 1) * GW]
        Bb = b_all[:, s * N:(s + 1) * N].astype(BF16)
        Cb = c_all[:, s * N:(s + 1) * N].astype(BF16)
        ex = jnp.dot(lhs, e_ref[s], preferred_element_type=F32)
        acum_x = ex[0:L] + ex[L:2 * L] + ex[2 * L:3 * L]
        dt_x = ex[3 * L:4 * L] + ex[4 * L:5 * L] + ex[5 * L:6 * L]

        xdt = xs * dt_x
        ht = ht_scr[s]
        y_inter = jnp.dot(Cb, ht.astype(BF16), preferred_element_type=F32) * jnp.exp(acum_x)

        b2 = jnp.concatenate([Bb] + pad + [Bb] + pad, axis=0)
        cb2 = lax.dot_general(Cb, b2, (((1,), (1,)), ((), ())), preferred_element_type=F32)
        y_parts = []
        for j in range(GW // LANES):
            sl = slice(j * LANES, (j + 1) * LANES)
            ax = acum_x[:, sl]
            v = jnp.sum(jnp.where(eye2, ax, 0.0), axis=0, keepdims=True)
            decay = jnp.exp(jnp.where(tri2, ax - v, -jnp.inf))
            m2 = (cb2 * decay).astype(BF16)
            xp = xdt[:, sl]
            top = jnp.where(r_idx == 0, xp, 0.0).astype(BF16)
            bot = jnp.where(r_idx == 1, xp, 0.0).astype(BF16)
            bd = jnp.concatenate([top] + pad + [bot] + pad, axis=0)
            y_parts.append(jnp.dot(m2, bd, preferred_element_type=F32))
        y = jnp.concatenate(y_parts, axis=1) + y_inter + dx_ref[:, s * GW:(s + 1) * GW] * xs

        y = y * _silu(z_ref[:, s * GW:(s + 1) * GW])
        y = y * lax.rsqrt(jnp.mean(y * y, axis=-1, keepdims=True) + NORM_EPS)
        y_ref[:, s * GW:(s + 1) * GW] = (y * ng_ref[:, s * GW:(s + 1) * GW]).astype(y_ref.dtype)

        last = acum_x[L - 1:L, :]
        xt = (xdt * jnp.exp(last - acum_x)).astype(BF16)
        upd = lax.dot_general(Bb, xt, (((0,), (0,)), ((), ())), preferred_element_type=F32)
        ht_new = ht * jnp.exp(last) + upd
        ht_scr[s] = ht_new

        @pl.when(c == pl.num_programs(2) - 1)
        def _():
            hout_ref[0, s * GW:(s + 1) * GW, :] = ht_new.T


def ssd_mixer(proj, dt, acum, conv_state, ssm_state, conv_w, conv_b, d_skip, norm_g,
              *, Bsz, T, L, d_inner, groups, n_state):
    M = Bsz * T
    heads = dt.shape[1]
    P = SSM_HEAD_DIM
    GW = d_inner // groups
    N = n_state
    R = GW // P
    nc = T // L
    ns = SSD_GROUPS_PER_STEP
    assert N == LANES and heads == LANES and GW % LANES == 0 and L % 16 == 0 and L <= P and groups % ns == 0
    SW, SN = ns * GW, ns * N
    xb, bb, cb = d_inner // SW, 2 * d_inner // SN, (2 * d_inner + groups * N) // SN
    kb, kc = d_inner // SN, (d_inner + groups * N) // SN

    expand = np.zeros((groups, heads, GW), np.float32)
    for g in range(groups):
        for r in range(R):
            expand[g, g * R + r, r * P:(r + 1) * P] = 1.0
    expand = jnp.asarray(expand, BF16)

    cbias = conv_b.reshape(1, -1)
    d_x = jnp.repeat(d_skip, P).reshape(1, d_inner)
    h0 = ssm_state.reshape(Bsz, heads * P, N)

    row = lambda b, g, c: b * nc + c
    in_specs = [
        pl.BlockSpec((L, SW), lambda b, g, c: (row(b, g, c), g)),
        pl.BlockSpec((L, SW), lambda b, g, c: (row(b, g, c), xb + g)),
        pl.BlockSpec((L, SN), lambda b, g, c: (row(b, g, c), bb + g)),
        pl.BlockSpec((L, SN), lambda b, g, c: (row(b, g, c), cb + g)),
        pl.BlockSpec((L, heads), lambda b, g, c: (row(b, g, c), 0)),
        pl.BlockSpec((L, heads), lambda b, g, c: (row(b, g, c), 0)),
        pl.BlockSpec((CONV_W, SW), lambda b, g, c: (0, g)),
        pl.BlockSpec((CONV_W, SN), lambda b, g, c: (0, kb + g)),
        pl.BlockSpec((CONV_W, SN), lambda b, g, c: (0, kc + g)),
        pl.BlockSpec((1, SW), lambda b, g, c: (0, g)),
        pl.BlockSpec((1, SN), lambda b, g, c: (0, kb + g)),
        pl.BlockSpec((1, SN), lambda b, g, c: (0, kc + g)),
        pl.BlockSpec((1, SW), lambda b, g, c: (0, g)),
        pl.BlockSpec((1, SW), lambda b, g, c: (0, g)),
        pl.BlockSpec((ns, heads, GW), lambda b, g, c: (g, 0, 0)),
        pl.BlockSpec((1, SW, N), lambda b, g, c: (b, g, 0)),
        pl.BlockSpec((1, CONV_W - 1, SW), lambda b, g, c: (b, 0, g)),
        pl.BlockSpec((1, CONV_W - 1, SN), lambda b, g, c: (b, 0, kb + g)),
        pl.BlockSpec((1, CONV_W - 1, SN), lambda b, g, c: (b, 0, kc + g)),
    ]
    out_specs = [
        pl.BlockSpec((L, SW), lambda b, g, c: (row(b, g, c), g)),
        pl.BlockSpec((1, SW, N), lambda b, g, c: (b, g, 0)),
        pl.BlockSpec((1, CONV_W - 1, SW), lambda b, g, c: (b, 0, g)),
        pl.BlockSpec((1, CONV_W - 1, SN), lambda b, g, c: (b, 0, g)),
        pl.BlockSpec((1, CONV_W - 1, SN), lambda b, g, c: (b, 0, g)),
    ]
    out_shape = [
        jax.ShapeDtypeStruct((M, d_inner), BF16),
        jax.ShapeDtypeStruct((Bsz, heads * P, N), F32),
        jax.ShapeDtypeStruct((Bsz, CONV_W - 1, d_inner), F32),
        jax.ShapeDtypeStruct((Bsz, CONV_W - 1, groups * N), F32),
        jax.ShapeDtypeStruct((Bsz, CONV_W - 1, groups * N), F32),
    ]
    y, h_new, sx, sb, sc = pl.pallas_call(
        functools.partial(_ssd_kernel, L=L, GW=GW),
        grid=(Bsz, groups // ns, nc),
        in_specs=in_specs,
        out_specs=out_specs,
        out_shape=out_shape,
        scratch_shapes=[pltpu.VMEM((ns, N, GW), F32),
                        pltpu.VMEM((8 + L, SW), F32),
                        pltpu.VMEM((8 + L, SN), F32),
                        pltpu.VMEM((8 + L, SN), F32)],
        compiler_params=_params("parallel", "parallel", "arbitrary"),
        name="ssd_mixer",
    )(proj, proj, proj, proj, dt, acum, conv_w, conv_w, conv_w, cbias, cbias, cbias,
      d_x, norm_g.reshape(1, d_inner), expand, h0, conv_state, conv_state, conv_state)
    new_conv = jnp.concatenate([sx, sb, sc], axis=-1)
    return y, new_conv, h_new.reshape(Bsz, heads, P, N)


def _fcum_kernel(past_ref, new_ref, o_ref, w_scr, *, past, Tn, heads):
    w_scr[...] = jnp.zeros_like(w_scr)
    if past:
        w_scr[0:past, 0:heads] = past_ref[0]
    w_scr[past:past + Tn, :] = new_ref[...]
    n = w_scr.shape[0]
    o_ref[0] = _cumsum_rows(w_scr[...], 1 << (n - 1).bit_length()).T


def fcum(cache_logf, logf_new, *, Bsz, Tn):
    past, heads = cache_logf.shape[1], cache_logf.shape[2]
    t_pad = -(-(past + Tn) // LANES) * LANES
    if past == 0:
        cache_logf = jnp.zeros((Bsz, 8, heads), F32)
    return pl.pallas_call(
        functools.partial(_fcum_kernel, past=past, Tn=Tn, heads=heads),
        grid=(Bsz,),
        in_specs=[pl.BlockSpec((1,) + cache_logf.shape[1:], lambda b: (b, 0, 0)),
                  pl.BlockSpec((Tn, LANES), lambda b: (b, 0))],
        out_specs=pl.BlockSpec((1, LANES, t_pad), lambda b: (b, 0, 0)),
        out_shape=jax.ShapeDtypeStruct((Bsz, LANES, t_pad), F32),
        scratch_shapes=[pltpu.VMEM((t_pad, LANES), F32)],
        compiler_params=_params("parallel"),
        name="fcum",
    )(cache_logf, logf_new)


def _row_to_cols(frow):
    return jnp.broadcast_to(frow, (LANES, frow.shape[1])).T


def _attn_prompt_kernel(q_ref, k_ref, v_ref, f_ref, o_ref, vt_scr, fk_scr, *, tb, hd, scale):
    T = q_ref.shape[0]
    nh = q_ref.shape[1] // hd
    nb = T // tb
    h0 = (pl.program_id(1) % (8 // nh)) * nh
    scale = scale * LOG2E
    frows = [f_ref[0, pl.ds(h0 + i, 1), :] * LOG2E for i in range(nh)]
    for i in range(nh):
        fk_scr[i] = _row_to_cols(frows[i])
        for j in range(nb):
            vt_scr[i, j] = v_ref[j * tb:(j + 1) * tb, i * hd:(i + 1) * hd].astype(F32).T.astype(BF16)
    kofs = lax.broadcasted_iota(jnp.int32, (tb, tb), 0)
    qofs = lax.broadcasted_iota(jnp.int32, (tb, tb), 1)

    for qi in range(nb):
        qs = slice(qi * tb, (qi + 1) * tb)
        q_t = [q_ref[qs, i * hd:(i + 1) * hd].astype(F32).T.astype(BF16) for i in range(nh)]

        def block(j, carry, masked):
            start = j * tb if isinstance(j, int) else pl.multiple_of(j * tb, tb)
            qk = [jnp.dot(k_ref[pl.ds(start, tb), i * hd:(i + 1) * hd], q_t[i], preferred_element_type=F32)
                  for i in range(nh)]
            out = []
            for i in range(nh):
                m, l, acc = carry[3 * i:3 * i + 3]
                fk = fk_scr[i, pl.ds(start, tb), :]
                s = qk[i] * scale
                s = s + frows[i][:, qs] - jnp.concatenate([fk] * (tb // LANES), axis=1)
                if masked:
                    s = jnp.where(kofs <= qofs, s, -jnp.inf)
                m_new = jnp.maximum(m, jnp.max(s, axis=0, keepdims=True))
                alpha = jnp.exp2(m - m_new)
                p = jnp.exp2(s - m_new)
                l = alpha * l + jnp.sum(p, axis=0, keepdims=True)
                acc = alpha * acc + jnp.dot(vt_scr[i, j], p.astype(BF16), preferred_element_type=F32)
                out += [m_new, l, acc]
            return tuple(out)

        carry = (jnp.full((1, tb), -jnp.inf, F32), jnp.zeros((1, tb), F32), jnp.zeros((hd, tb), F32)) * nh
        if qi:
            carry = lax.fori_loop(0, qi, functools.partial(block, masked=False), carry)
        carry = block(qi, carry, True)
        for i in range(nh):
            o_ref[qs, i * hd:(i + 1) * hd] = (carry[3 * i + 2] / carry[3 * i + 1]).T.astype(o_ref.dtype)


def attn_prompt(q, k, v, f_t, *, Bsz, T, heads, hd):
    tb = _tile(T, ATTN_BLOCK)
    nh = ATTN_HEADS_PER_STEP
    assert tb % LANES == 0 and heads % 8 == 0 and 8 % nh == 0 and hd == LANES
    blk = pl.BlockSpec((T, nh * hd), lambda b, h: (b, h))
    return pl.pallas_call(
        functools.partial(_attn_prompt_kernel, tb=tb, hd=hd, scale=hd ** -0.5),
        grid=(Bsz, heads // nh),
        in_specs=[blk, blk, blk, pl.BlockSpec((1, 8, T), lambda b, h: (b, h * nh // 8, 0))],
        out_specs=blk,
        out_shape=jax.ShapeDtypeStruct((Bsz * T, heads * hd), BF16),
        scratch_shapes=[pltpu.VMEM((nh, T // tb, hd, tb), BF16), pltpu.VMEM((nh, T, LANES), F32)],
        compiler_params=_params("parallel", "parallel"),
        name="attn_prompt",
    )(q, k, v, f_t)


def _attn_sample_kernel(q_ref, kc_ref, vc_ref, kn_ref, vn_ref, f_ref, o_ref, *, Tn, past, hd, nh, scale):
    zrows = jnp.zeros((LANES - Tn, hd), BF16)
    rowi = lax.broadcasted_iota(jnp.int32, (Tn, LANES), 0)
    coli = lax.broadcasted_iota(jnp.int32, (Tn, LANES), 1)
    nt = (((1,), (1,)), ((), ()))
    for hh in range(nh):
        sl = slice(hh * hd, (hh + 1) * hd)
        q = q_ref[:, sl]
        kc = kc_ref[0, :, sl].astype(BF16)
        vc = vc_ref[0, :, sl].astype(BF16)
        kn = jnp.concatenate([kn_ref[:, sl], zrows], axis=0)
        vn = jnp.concatenate([vn_ref[:, sl], zrows], axis=0)
        f_past = f_ref[0, hh:hh + 1, 0:past]
        f_new = f_ref[0, hh:hh + 1, past:past + LANES]
        fq_col = _row_to_cols(f_new)[0:Tn, 0:1]
        s1 = lax.dot_general(q, kc, nt, preferred_element_type=F32) * scale + fq_col - f_past
        s2 = lax.dot_general(q, kn, nt, preferred_element_type=F32) * scale + fq_col - f_new
        s2 = jnp.where(coli <= rowi, s2, -jnp.inf)
        m = jnp.maximum(jnp.max(s1, axis=1, keepdims=True), jnp.max(s2, axis=1, keepdims=True))
        p1 = jnp.exp(s1 - m)
        p2 = jnp.exp(s2 - m)
        l = jnp.sum(p1, axis=1, keepdims=True) + jnp.sum(p2, axis=1, keepdims=True)
        o = jnp.dot(p1.astype(BF16), vc, preferred_element_type=F32)
        o = o + jnp.dot(p2.astype(BF16), vn, preferred_element_type=F32)
        o_ref[:, sl] = (o / l).astype(o_ref.dtype)


def attn_sample(q, cache_k, cache_v, k_new, v_new, f_t, *, Bsz, Tn, heads, hd):
    past = cache_k.shape[1]
    nh = 8
    assert past % LANES == 0 and Tn % 16 == 0 and Tn <= LANES and heads % nh == 0
    ck = cache_k.reshape(Bsz, past, heads * hd)
    cv = cache_v.reshape(Bsz, past, heads * hd)
    return pl.pallas_call(
        functools.partial(_attn_sample_kernel, Tn=Tn, past=past, hd=hd, nh=nh, scale=hd ** -0.5),
        grid=(Bsz, heads // nh),
        in_specs=[pl.BlockSpec((Tn, nh * hd), lambda b, g: (b, g)),
                  pl.BlockSpec((1, past, nh * hd), lambda b, g: (b, 0, g)),
                  pl.BlockSpec((1, past, nh * hd), lambda b, g: (b, 0, g)),
                  pl.BlockSpec((Tn, nh * hd), lambda b, g: (b, g)),
                  pl.BlockSpec((Tn, nh * hd), lambda b, g: (b, g)),
                  pl.BlockSpec((1, nh, f_t.shape[2]), lambda b, g: (b, g, 0))],
        out_specs=pl.BlockSpec((Tn, nh * hd), lambda b, g: (b, g)),
        out_shape=jax.ShapeDtypeStruct((Bsz * Tn, heads * hd), BF16),
        compiler_params=_params("parallel", "parallel"),
        name="attn_sample",
    )(q, ck, cv, k_new, v_new, f_t)


def _trunk(x, conv_st, ssm_st, cache_k, cache_v, cache_logf, w, p):
    Bsz, T, D = x.shape
    M = Bsz * T
    heads, hd = cache_k.shape[2], cache_k.shape[3]
    d_inner = p["a_norm"].shape[-1]
    d_ff = p["d_ff"]
    n_state = ssm_st.shape[-1]
    groups = (conv_st.shape[-1] - d_inner) // (2 * n_state)
    d_zxbc = 2 * d_inner + 2 * groups * n_state
    L = SSD_CHUNK if T % SSD_CHUNK == 0 else T
    small = M <= 512
    wb = {}

    norm_ffn = p["norm_ffn"].reshape(-1, D)
    tn_r = 256 if small else 512

    def ffn(h, xn, idx, next_gains):
        a, wb[f"wg{idx}"], wb[f"wu{idx}"] = mm_gateup(xn, w[f"wg{idx}"], w[f"wu{idx}"], n_cols=d_ff)
        h, xns, wb[f"wd{idx}"] = mm_resid(a, w[f"wd{idx}"], h, 0.5, next_gains, tn=tn_r, name="mm_down")
        return h, xns

    h = x.reshape(M, D)
    h, (u,) = ffn(h, norm_prep(h, norm_ffn[0]), 0, [p["norm_mix"][0]])
    (proj,), wb["w_in"] = mm_plain(u, w["w_in"], [F32], n_cols=d_zxbc, tn=512 if small else 1024,
                                   name="mm_in_proj")
    dt, acum, wb["w_dt"] = mm_dt(u, w["w_dt"], p["a_dt_bias"][0], p["a_A_log"][0], L=L)
    y, new_conv, new_ssm = ssd_mixer(
        proj, dt, acum, conv_st[0], ssm_st[0], p["a_conv_w"][0], p["a_conv_b"][0], p["a_D"][0], p["a_norm"][0],
        Bsz=Bsz, T=T, L=L, d_inner=d_inner, groups=groups, n_state=n_state)
    h, (xn,), wb["w_out"] = mm_resid(y, w["w_out"], h, 1.0, [norm_ffn[1]], tn=tn_r, name="mm_out_proj")
    h, (x_kv, xn) = ffn(h, xn, 1, [p["kv_norm"], norm_ffn[2]])
    tn = 512 if small else 1024
    (k32, kb), wb["w_k"] = mm_plain(x_kv, w["w_k"], [F32, BF16], n_cols=D, tn=tn, name="mm_k")
    (v32, vb), wb["w_v"] = mm_plain(x_kv, w["w_v"], [F32, BF16], n_cols=D, tn=tn, name="mm_v")
    logf = mm_logf(x_kv, p["w_f"], p["b_f"])
    f_t = fcum(cache_logf, logf, Bsz=Bsz, Tn=T)
    h, (u,) = ffn(h, xn, 2, [p["norm_mix"][1]])
    (q,), wb["w_q"] = mm_plain(u, w["w_q"], [BF16], n_cols=D, tn=tn, name="mm_q")
    if cache_k.shape[1] == 0:
        o = attn_prompt(q, kb, vb, f_t, Bsz=Bsz, T=T, heads=heads, hd=hd)
    else:
        o = attn_sample(q, cache_k, cache_v, kb, vb, f_t, Bsz=Bsz, Tn=T, heads=heads, hd=hd)
    h, (xn,), wb["w_o"] = mm_resid(o, w["w_o"], h, 1.0, [norm_ffn[3]], name="mm_wo")
    h, _ = ffn(h, xn, 3, [])
    y_out = rmsnorm(h, p["norm_final"], F32).reshape(Bsz, T, D)
    outs = (y_out, new_conv[None], new_ssm[None],
            k32.reshape(Bsz, T, heads, hd), v32.reshape(Bsz, T, heads, hd),
            logf[:, :heads].reshape(Bsz, T, heads))
    return outs, wb


def kernel(x_prompt, x_sample, state_conv, state_ssm, cache_k, cache_v, cache_logf, norm_ffn, ffn_wg, ffn_wu,
           ffn_wd, norm_mix, a_w_in, a_conv_w, a_conv_b, a_dt_bias, a_A_log, a_D, a_norm, a_w_out, kv_norm,
           w_kvf, b_f, b_wq, b_wo, norm_final):
    assert norm_mix.shape[0] == 2 and a_w_in.shape[0] == 1 and b_wq.shape[0] == 1
    D = x_prompt.shape[-1]
    F = ffn_wg.shape[-1]
    heads = cache_k.shape[2]
    d_zxbc = a_w_in.shape[-1] - a_dt_bias.shape[-1]
    p = {"norm_ffn": norm_ffn, "norm_mix": norm_mix, "a_conv_w": a_conv_w, "a_conv_b": a_conv_b,
         "a_dt_bias": a_dt_bias, "a_A_log": a_A_log, "a_D": a_D, "a_norm": a_norm, "kv_norm": kv_norm,
         "norm_final": norm_final, "d_ff": F,
         "w_f": jnp.pad(w_kvf[:, 2 * D:], ((0, 0), (0, LANES - heads))).astype(BF16),
         "b_f": jnp.pad(b_f, (0, LANES - heads)).reshape(1, LANES)}
    wg, wu, wd = ffn_wg.reshape(-1, F), ffn_wu.reshape(-1, F), ffn_wd.reshape(-1, D)
    w = {"w_in": (a_w_in[0], 0, 0), "w_dt": (a_w_in[0], 0, d_zxbc), "w_out": (a_w_out[0], 0, 0),
         "w_k": (w_kvf, 0, 0), "w_v": (w_kvf, 0, D), "w_q": (b_wq[0], 0, 0), "w_o": (b_wo[0], 0, 0)}
    for idx in range(wg.shape[0] // D):
        w[f"wg{idx}"], w[f"wu{idx}"], w[f"wd{idx}"] = (wg, idx, 0), (wu, idx, 0), (wd, idx, 0)

    bp = x_prompt.shape[0]
    zero_conv = jnp.zeros((state_conv.shape[0], bp) + state_conv.shape[2:], state_conv.dtype)
    zero_ssm = jnp.zeros((state_ssm.shape[0], bp) + state_ssm.shape[2:], state_ssm.dtype)
    empty_kv = jnp.zeros((bp, 0) + cache_k.shape[2:], cache_k.dtype)
    empty_lf = jnp.zeros((bp, 0, heads), cache_logf.dtype)
    (y_s, s_conv, s_ssm, s_k, s_v, s_logf), wb = _trunk(
        x_sample, state_conv, state_ssm, cache_k, cache_v, cache_logf, w, p)
    (y_p, p_conv, p_ssm, p_k, p_v, p_logf), _ = _trunk(
        x_prompt, zero_conv, zero_ssm, empty_kv, empty_kv, empty_lf, wb, p)
    return (y_p, y_s, p_conv, p_ssm, p_k, p_v, p_logf, s_conv, s_ssm, s_k, s_v, s_logf)
```

```python
import functools

import numpy as np
import jax
import jax.numpy as jnp
from jax import lax
from jax.experimental import pallas as pl
from jax.experimental.pallas import tpu as pltpu

NORM_EPS = 1e-5
SSD_CHUNK = 64
SSM_HEAD_DIM = 64
CONV_W = 4
LANES = 128
LOG2E = 1.4426950408889634
VMEM_LIMIT_BYTES = 56 * 1024 * 1024
SSD_GROUPS_PER_STEP = 4
ATTN_BLOCK = 256
ATTN_HEADS_PER_STEP = 4
BF16 = jnp.bfloat16
F32 = jnp.float32


def _params(*sem):
    return pltpu.CompilerParams(dimension_semantics=sem, vmem_limit_bytes=VMEM_LIMIT_BYTES)


def _tile(n, pref):
    if n <= pref:
        return n
    t = 1 << (pref.bit_length() - 1)
    while n % t:
        t //= 2
    return t


def _sigmoid(x):
    return 1.0 / (1.0 + jnp.exp(-x))


def _silu(x):
    return x * _sigmoid(x)


def _softplus(x):
    return jnp.maximum(x, 0.0) + jnp.log1p(jnp.exp(-jnp.abs(x)))


def _cumsum_rows(x, seg):
    rowi = jnp.bitwise_and(lax.broadcasted_iota(jnp.int32, x.shape, 0), seg - 1)
    s = 1
    while s < seg:
        x = x + jnp.where(rowi >= s, pltpu.roll(x, s, 0), 0.0)
        s *= 2
    return x


def _rmsnorm_kernel(x_ref, g_ref, o_ref):
    x = x_ref[...]
    y = x * lax.rsqrt(jnp.mean(x * x, axis=-1, keepdims=True) + NORM_EPS)
    o_ref[...] = (y * g_ref[...]).astype(o_ref.dtype)


def rmsnorm(x, g, out_dtype):
    M, D = x.shape
    tm = _tile(M, 512)
    return pl.pallas_call(
        _rmsnorm_kernel,
        grid=(M // tm,),
        in_specs=[pl.BlockSpec((tm, D), lambda i: (i, 0)),
                  pl.BlockSpec((1, D), lambda i: (0, 0))],
        out_specs=pl.BlockSpec((tm, D), lambda i: (i, 0)),
        out_shape=jax.ShapeDtypeStruct((M, D), out_dtype),
        compiler_params=_params("parallel"),
        name="rmsnorm",
    )(x, g.reshape(1, D))


def _fold_lanes(v):
    parts = [v[:, k * LANES:(k + 1) * LANES] for k in range(v.shape[1] // LANES)]
    while len(parts) > 1:
        parts = [a + b for a, b in zip(parts[::2], parts[1::2])] + parts[len(parts) & ~1:]
    return parts[0]


def _row_scale(ssq, d):
    r = lax.rsqrt(jnp.sum(ssq, axis=-1, keepdims=True) * (1.0 / d) + NORM_EPS)
    return jnp.broadcast_to(r, ssq.shape)


def _scale_rows(acc, rs_ref):
    return acc * jnp.concatenate([rs_ref[...]] * (acc.shape[1] // LANES), axis=1)


def _norm_prep_kernel(x_ref, g_ref, hb_ref, rs_ref):
    x = x_ref[...]
    hb_ref[...] = (x * g_ref[...]).astype(hb_ref.dtype)
    rs_ref[...] = _row_scale(_fold_lanes(x * x), x.shape[1])


def norm_prep(x, g):
    M, D = x.shape
    tm = _tile(M, 512)
    hb, rs = pl.pallas_call(
        _norm_prep_kernel,
        grid=(M // tm,),
        in_specs=[pl.BlockSpec((tm, D), lambda i: (i, 0)),
                  pl.BlockSpec((1, D), lambda i: (0, 0))],
        out_specs=[pl.BlockSpec((tm, D), lambda i: (i, 0)),
                   pl.BlockSpec((tm, LANES), lambda i: (i, 0))],
        out_shape=[jax.ShapeDtypeStruct((M, D), BF16), jax.ShapeDtypeStruct((M, LANES), F32)],
        compiler_params=_params("parallel"),
        name="norm_prep",
    )(x, g.reshape(1, D))
    return hb, rs


def _w_bf16(w_ref, wb_ref):
    w = w_ref[...]
    if wb_ref is not None:
        w = w.astype(BF16)
        wb_ref[...] = w
    return w


def _w_specs(w, K, n_cols, tn, M, tm):
    arr, row_blk, col_off = w
    emit = arr.dtype != BF16
    assert col_off % tn == 0 and n_cols % tn == 0
    assert not emit or M == tm
    cb = col_off // tn
    in_spec = pl.BlockSpec((K, tn), lambda i, j: (row_blk, cb + j))
    out_specs = [pl.BlockSpec((K, tn), lambda i, j: (0, j))] if emit else []
    out_shapes = [jax.ShapeDtypeStruct((K, n_cols), BF16)] if emit else []
    return arr, emit, in_spec, out_specs, out_shapes


def _x_specs(xn, tm):
    hb, _ = xn
    return [pl.BlockSpec((tm, hb.shape[1]), lambda i, j: (i, 0)),
            pl.BlockSpec((tm, LANES), lambda i, j: (i, 0))]


def _mm_plain_kernel(x_ref, rs_ref, w_ref, *o_refs, emit):
    wb_ref = o_refs[-1] if emit else None
    acc = jnp.dot(x_ref[...], _w_bf16(w_ref, wb_ref), preferred_element_type=F32)
    acc = _scale_rows(acc, rs_ref)
    for o_ref in o_refs[:len(o_refs) - emit]:
        o_ref[...] = acc.astype(o_ref.dtype)


def mm_plain(xn, w, out_dtypes, *, n_cols, tm=1024, tn=1024, name="mm_plain"):
    M, K = xn[0].shape
    tm, tn = _tile(M, tm), _tile(n_cols, tn)
    arr, emit, w_spec, wb_specs, wb_shapes = _w_specs(w, K, n_cols, tn, M, tm)
    outs = pl.pallas_call(
        functools.partial(_mm_plain_kernel, emit=emit),
        grid=(M // tm, n_cols // tn),
        in_specs=_x_specs(xn, tm) + [w_spec],
        out_specs=[pl.BlockSpec((tm, tn), lambda i, j: (i, j)) for _ in out_dtypes] + wb_specs,
        out_shape=[jax.ShapeDtypeStruct((M, n_cols), dt) for dt in out_dtypes] + wb_shapes,
        compiler_params=_params("parallel", "arbitrary"),
        name=name,
    )(*xn, arr)
    n = len(out_dtypes)
    return outs[:n], ((outs[n], 0, 0) if emit else w)


def _mm_resid_kernel(*refs, alpha, n_norm, emit, n_total):
    x_ref, w_ref, r_ref = refs[:3]
    g_refs = refs[3:3 + n_norm]
    o_ref = refs[3 + n_norm]
    hb_refs = refs[4 + n_norm:4 + 2 * n_norm]
    rs_ref = refs[4 + 2 * n_norm] if n_norm else None
    wb_ref = refs[-1] if emit else None
    acc = jnp.dot(x_ref[...], _w_bf16(w_ref, wb_ref), preferred_element_type=F32)
    h = r_ref[...] + alpha * acc
    o_ref[...] = h
    for g_ref, hb_ref in zip(g_refs, hb_refs):
        hb_ref[...] = (h * g_ref[...]).astype(hb_ref.dtype)
    if n_norm:
        part = _fold_lanes(h * h)
        j = pl.program_id(1)

        @pl.when(j == 0)
        def _():
            rs_ref[...] = part

        @pl.when(j > 0)
        def _():
            rs_ref[...] += part

        @pl.when(j == pl.num_programs(1) - 1)
        def _():
            rs_ref[...] = _row_scale(rs_ref[...], n_total)


def mm_resid(x, w, resid, alpha, gains=(), *, tm=512, tn=512, name="mm_resid"):
    M, K = x.shape
    N = resid.shape[1]
    tm, tn = _tile(M, tm), _tile(N, tn)
    n_norm = len(gains)
    arr, emit, w_spec, wb_specs, wb_shapes = _w_specs(w, K, N, tn, M, tm)
    tile = pl.BlockSpec((tm, tn), lambda i, j: (i, j))
    outs = pl.pallas_call(
        functools.partial(_mm_resid_kernel, alpha=alpha, n_norm=n_norm, emit=emit, n_total=N),
        grid=(M // tm, N // tn),
        in_specs=[pl.BlockSpec((tm, K), lambda i, j: (i, 0)), w_spec, tile]
                 + [pl.BlockSpec((1, tn), lambda i, j: (0, j))] * n_norm,
        out_specs=[tile] * (1 + n_norm) + [pl.BlockSpec((tm, LANES), lambda i, j: (i, 0))] * bool(n_norm) + wb_specs,
        out_shape=[jax.ShapeDtypeStruct((M, N), F32)] + [jax.ShapeDtypeStruct((M, N), BF16)] * n_norm
                  + [jax.ShapeDtypeStruct((M, LANES), F32)] * bool(n_norm) + wb_shapes,
        compiler_params=_params("parallel", "arbitrary"),
        name=name,
    )(x, arr, resid, *[g.reshape(1, N) for g in gains])
    xns = [(outs[1 + k], outs[1 + n_norm]) for k in range(n_norm)]
    return outs[0], xns, ((outs[-1], 0, 0) if emit else w)


def _mm_gateup_kernel(x_ref, rs_ref, wg_ref, wu_ref, o_ref, *wb_refs):
    x = x_ref[...]
    g = jnp.dot(x, _w_bf16(wg_ref, wb_refs[0] if wb_refs else None), preferred_element_type=F32)
    u = jnp.dot(x, _w_bf16(wu_ref, wb_refs[1] if wb_refs else None), preferred_element_type=F32)
    o_ref[...] = (_silu(_scale_rows(g, rs_ref)) * _scale_rows(u, rs_ref)).astype(o_ref.dtype)


def mm_gateup(xn, wg, wu, *, n_cols, tm=2048, tn=256):
    M, K = xn[0].shape
    tm, tn = _tile(M, tm), _tile(n_cols, tn)
    garr, emit, g_spec, gb_specs, gb_shapes = _w_specs(wg, K, n_cols, tn, M, tm)
    uarr, _, u_spec, ub_specs, ub_shapes = _w_specs(wu, K, n_cols, tn, M, tm)
    outs = pl.pallas_call(
        _mm_gateup_kernel,
        grid=(M // tm, n_cols // tn),
        in_specs=_x_specs(xn, tm) + [g_spec, u_spec],
        out_specs=[pl.BlockSpec((tm, tn), lambda i, j: (i, j))] + gb_specs + ub_specs,
        out_shape=[jax.ShapeDtypeStruct((M, n_cols), BF16)] + gb_shapes + ub_shapes,
        compiler_params=_params("parallel", "arbitrary"),
        name="mm_gateup",
    )(*xn, garr, uarr)
    if emit:
        return outs[0], (outs[1], 0, 0), (outs[2], 0, 0)
    return outs[0], wg, wu


def _mm_dt_kernel(x_ref, rs_ref, w_ref, b_ref, alog_ref, dt_ref, acum_ref, *wb_refs, L):
    raw = jnp.dot(x_ref[...], _w_bf16(w_ref, wb_refs[0] if wb_refs else None), preferred_element_type=F32)
    dt = _softplus(_scale_rows(raw, rs_ref) + b_ref[...])
    dt_ref[...] = dt
    acum_ref[...] = _cumsum_rows(dt * (-jnp.exp(alog_ref[...])), L)


def mm_dt(xn, w, dt_bias, a_log, *, L, tm=2048):
    M, K = xn[0].shape
    heads = dt_bias.shape[0]
    tm = _tile(M, tm)
    assert tm % L == 0 and L & (L - 1) == 0
    arr, emit, w_spec, wb_specs, wb_shapes = _w_specs(w, K, heads, heads, M, tm)
    vec = pl.BlockSpec((1, heads), lambda i, j: (0, 0))
    outs = pl.pallas_call(
        functools.partial(_mm_dt_kernel, L=L),
        grid=(M // tm, 1),
        in_specs=_x_specs(xn, tm) + [w_spec, vec, vec],
        out_specs=[pl.BlockSpec((tm, heads), lambda i, j: (i, 0))] * 2 + wb_specs,
        out_shape=[jax.ShapeDtypeStruct((M, heads), F32)] * 2 + wb_shapes,
        compiler_params=_params("parallel", "arbitrary"),
        name="mm_dt",
    )(*xn, arr, dt_bias.reshape(1, heads), a_log.reshape(1, heads))
    return outs[0], outs[1], ((outs[2], 0, 0) if emit else w)


def _mm_logf_kernel(x_ref, rs_ref, w_ref, b_ref, o_ref):
    s = jnp.dot(x_ref[...], w_ref[...], preferred_element_type=F32)
    s = _scale_rows(s, rs_ref) + b_ref[...]
    o_ref[...] = -_softplus(-s)


def mm_logf(xn, w, b):
    M, K = xn[0].shape
    N = w.shape[1]
    tm = _tile(M, 2048)
    return pl.pallas_call(
        _mm_logf_kernel,
        grid=(M // tm, 1),
        in_specs=_x_specs(xn, tm) + [pl.BlockSpec((K, N), lambda i, j: (0, 0)),
                                     pl.BlockSpec((1, N), lambda i, j: (0, 0))],
        out_specs=pl.BlockSpec((tm, N), lambda i, j: (i, 0)),
        out_shape=jax.ShapeDtypeStruct((M, N), F32),
        compiler_params=_params("parallel", "arbitrary"),
        name="mm_logf",
    )(*xn, w, b)


def _split3(v):
    hi = v.astype(BF16)
    r1 = v - hi.astype(F32)
    mid = r1.astype(BF16)
    lo = (r1 - mid.astype(F32)).astype(BF16)
    return [hi, mid, lo]


def _ssd_kernel(z_ref, x_ref, b_ref, c_ref, dt_ref, acum_ref,
                cwx_ref, cwb_ref, cwc_ref, cbx_ref, cbb_ref, cbc_ref,
                dx_ref, ng_ref, e_ref,
                h0_ref, sx0_ref, sb0_ref, sc0_ref,
                y_ref, hout_ref, sxo_ref, sbo_ref, sco_ref,
                ht_scr, xp_scr, bp_scr, cp_scr, *, L, GW):
    c = pl.program_id(2)
    P = SSM_HEAD_DIM
    N = LANES
    n_sub = x_ref.shape[1] // GW

    @pl.when(c == 0)
    def _():
        for s in range(n_sub):
            ht_scr[s] = h0_ref[0, s * GW:(s + 1) * GW, :].T
        xp_scr[5:8, :] = sx0_ref[0]
        bp_scr[5:8, :] = sb0_ref[0]
        cp_scr[5:8, :] = sc0_ref[0]

    def conv_silu(v_ref, p_scr, w_ref, bias_ref):
        p_scr[8:8 + L, :] = v_ref[...]
        acc = bias_ref[...] + p_scr[5:5 + L, :] * w_ref[0:1, :]
        for k in range(1, CONV_W):
            acc = acc + p_scr[5 + k:5 + k + L, :] * w_ref[k:k + 1, :]
        p_scr[5:8, :] = p_scr[5 + L:8 + L, :]
        return _silu(acc)

    xs_all = conv_silu(x_ref, xp_scr, cwx_ref, cbx_ref)
    b_all = conv_silu(b_ref, bp_scr, cwb_ref, cbb_ref)
    c_all = conv_silu(c_ref, cp_scr, cwc_ref, cbc_ref)
    sxo_ref[0] = xp_scr[5:8, :]
    sbo_ref[0] = bp_scr[5:8, :]
    sco_ref[0] = cp_scr[5:8, :]

    lhs = jnp.concatenate(_split3(acum_ref[...]) + _split3(dt_ref[...]), axis=0)

    lane = lax.broadcasted_iota(jnp.int32, (L, LANES), 1)
    rowi = lax.broadcasted_iota(jnp.int32, (L, LANES), 0)
    s_idx = jnp.bitwise_and(lane, P - 1)
    r_idx = jnp.right_shift(lane, P.bit_length() - 1)
    eye2 = rowi == s_idx
    tri2 = s_idx <= rowi
    pad = [] if L == P else [jnp.zeros((P - L, LANES), BF16)]

    for s in range(n_sub):
        xs = xs_all[:, s * GW:(s + 1) * GW]
        Bb = b_all[:, s * N:(s + 1) * N].astype(BF16)
        Cb = c_all[:, s * N:(s + 1) * N].astype(BF16)
        ex = jnp.dot(lhs, e_ref[s], preferred_element_type=F32)
        acum_x = ex[0:L] + ex[L:2 * L] + ex[2 * L:3 * L]
        dt_x = ex[3 * L:4 * L] + ex[4 * L:5 * L] + ex[5 * L:6 * L]

        xdt = xs * dt_x
        ht = ht_scr[s]
        y_inter = jnp.dot(Cb, ht.astype(BF16), preferred_element_type=F32) * jnp.exp(acum_x)

        b2 = jnp.concatenate([Bb] + pad + [Bb] + pad, axis=0)
        cb2 = lax.dot_general(Cb, b2, (((1,), (1,)), ((), ())), preferred_element_type=F32)
        y_parts = []
        for j in range(GW // LANES):
            sl = slice(j * LANES, (j + 1) * LANES)
            ax = acum_x[:, sl]
            v = jnp.sum(jnp.where(eye2, ax, 0.0), axis=0, keepdims=True)
            decay = jnp.exp(jnp.where(tri2, ax - v, -jnp.inf))
            m2 = (cb2 * decay).astype(BF16)
            xp = xdt[:, sl]
            top = jnp.where(r_idx == 0, xp, 0.0).astype(BF16)
            bot = jnp.where(r_idx == 1, xp, 0.0).astype(BF16)
            bd = jnp.concatenate([top] + pad + [bot] + pad, axis=0)
            y_parts.append(jnp.dot(m2, bd, preferred_element_type=F32))
        y = jnp.concatenate(y_parts, axis=1) + y_inter + dx_ref[:, s * GW:(s + 1) * GW] * xs

        y = y * _silu(z_ref[:, s * GW:(s + 1) * GW])
        y = y * lax.rsqrt(jnp.mean(y * y, axis=-1, keepdims=True) + NORM_EPS)
        y_ref[:, s * GW:(s + 1) * GW] = (y * ng_ref[:, s * GW:(s + 1) * GW]).astype(y_ref.dtype)

        last = acum_x[L - 1:L, :]
        xt = (xdt * jnp.exp(last - acum_x)).astype(BF16)
        upd = lax.dot_general(Bb, xt, (((0,), (0,)), ((), ())), preferred_element_type=F32)
        ht_new = ht * jnp.exp(last) + upd
        ht_scr[s] = ht_new

        @pl.when(c == pl.num_programs(2) - 1)
        def _():
            hout_ref[0, s * GW:(s + 1) * GW, :] = ht_new.T


def ssd_mixer(proj, dt, acum, conv_state, ssm_state, conv_w, conv_b, d_skip, norm_g,
              *, Bsz, T, L, d_inner, groups, n_state):
    M = Bsz * T
    heads = dt.shape[1]
    P = SSM_HEAD_DIM
    GW = d_inner // groups
    N = n_state
    R = GW // P
    nc = T // L
    ns = SSD_GROUPS_PER_STEP
    assert N == LANES and heads == LANES and GW % LANES == 0 and L % 16 == 0 and L <= P and groups % ns == 0
    SW, SN = ns * GW, ns * N
    xb, bb, cb = d_inner // SW, 2 * d_inner // SN, (2 * d_inner + groups * N) // SN
    kb, kc = d_inner // SN, (d_inner + groups * N) // SN

    expand = np.zeros((groups, heads, GW), np.float32)
    for g in range(groups):
        for r in range(R):
            expand[g, g * R + r, r * P:(r + 1) * P] = 1.0
    expand = jnp.asarray(expand, BF16)

    cbias = conv_b.reshape(1, -1)
    d_x = jnp.repeat(d_skip, P).reshape(1, d_inner)
    h0 = ssm_state.reshape(Bsz, heads * P, N)

    row = lambda b, g, c: b * nc + c
    in_specs = [
        pl.BlockSpec((L, SW), lambda b, g, c: (row(b, g, c), g)),
        pl.BlockSpec((L, SW), lambda b, g, c: (row(b, g, c), xb + g)),
        pl.BlockSpec((L, SN), lambda b, g, c: (row(b, g, c), bb + g)),
        pl.BlockSpec((L, SN), lambda b, g, c: (row(b, g, c), cb + g)),
        pl.BlockSpec((L, heads), lambda b, g, c: (row(b, g, c), 0)),
        pl.BlockSpec((L, heads), lambda b, g, c: (row(b, g, c), 0)),
        pl.BlockSpec((CONV_W, SW), lambda b, g, c: (0, g)),
        pl.BlockSpec((CONV_W, SN), lambda b, g, c: (0, kb + g)),
        pl.BlockSpec((CONV_W, SN), lambda b, g, c: (0, kc + g)),
        pl.BlockSpec((1, SW), lambda b, g, c: (0, g)),
        pl.BlockSpec((1, SN), lambda b, g, c: (0, kb + g)),
        pl.BlockSpec((1, SN), lambda b, g, c: (0, kc + g)),
        pl.BlockSpec((1, SW), lambda b, g, c: (0, g)),
        pl.BlockSpec((1, SW), lambda b, g, c: (0, g)),
        pl.BlockSpec((ns, heads, GW), lambda b, g, c: (g, 0, 0)),
        pl.BlockSpec((1, SW, N), lambda b, g, c: (b, g, 0)),
        pl.BlockSpec((1, CONV_W - 1, SW), lambda b, g, c: (b, 0, g)),
        pl.BlockSpec((1, CONV_W - 1, SN), lambda b, g, c: (b, 0, kb + g)),
        pl.BlockSpec((1, CONV_W - 1, SN), lambda b, g, c: (b, 0, kc + g)),
    ]
    out_specs = [
        pl.BlockSpec((L, SW), lambda b, g, c: (row(b, g, c), g)),
        pl.BlockSpec((1, SW, N), lambda b, g, c: (b, g, 0)),
        pl.BlockSpec((1, CONV_W - 1, SW), lambda b, g, c: (b, 0, g)),
        pl.BlockSpec((1, CONV_W - 1, SN), lambda b, g, c: (b, 0, g)),
        pl.BlockSpec((1, CONV_W - 1, SN), lambda b, g, c: (b, 0, g)),
    ]
    out_shape = [
        jax.ShapeDtypeStruct((M, d_inner), BF16),
        jax.ShapeDtypeStruct((Bsz, heads * P, N), F32),
        jax.ShapeDtypeStruct((Bsz, CONV_W - 1, d_inner), F32),
        jax.ShapeDtypeStruct((Bsz, CONV_W - 1, groups * N), F32),
        jax.ShapeDtypeStruct((Bsz, CONV_W - 1, groups * N), F32),
    ]
    y, h_new, sx, sb, sc = pl.pallas_call(
        functools.partial(_ssd_kernel, L=L, GW=GW),
        grid=(Bsz, groups // ns, nc),
        in_specs=in_specs,
        out_specs=out_specs,
        out_shape=out_shape,
        scratch_shapes=[pltpu.VMEM((ns, N, GW), F32),
                        pltpu.VMEM((8 + L, SW), F32),
                        pltpu.VMEM((8 + L, SN), F32),
                        pltpu.VMEM((8 + L, SN), F32)],
        compiler_params=_params("parallel", "parallel", "arbitrary"),
        name="ssd_mixer",
    )(proj, proj, proj, proj, dt, acum, conv_w, conv_w, conv_w, cbias, cbias, cbias,
      d_x, norm_g.reshape(1, d_inner), expand, h0, conv_state, conv_state, conv_state)
    new_conv = jnp.concatenate([sx, sb, sc], axis=-1)
    return y, new_conv, h_new.reshape(Bsz, heads, P, N)


def _fcum_kernel(past_ref, new_ref, o_ref, w_scr, *, past, Tn, heads):
    w_scr[...] = jnp.zeros_like(w_scr)
    if past:
        w_scr[0:past, 0:heads] = past_ref[0]
    w_scr[past:past + Tn, :] = new_ref[...]
    n = w_scr.shape[0]
    o_ref[0] = _cumsum_rows(w_scr[...], 1 << (n - 1).bit_length()).T


def fcum(cache_logf, logf_new, *, Bsz, Tn):
    past, heads = cache_logf.shape[1], cache_logf.shape[2]
    t_pad = -(-(past + Tn) // LANES) * LANES
    if past == 0:
        cache_logf = jnp.zeros((Bsz, 8, heads), F32)
    return pl.pallas_call(
        functools.partial(_fcum_kernel, past=past, Tn=Tn, heads=heads),
        grid=(Bsz,),
        in_specs=[pl.BlockSpec((1,) + cache_logf.shape[1:], lambda b: (b, 0, 0)),
                  pl.BlockSpec((Tn, LANES), lambda b: (b, 0))],
        out_specs=pl.BlockSpec((1, LANES, t_pad), lambda b: (b, 0, 0)),
        out_shape=jax.ShapeDtypeStruct((Bsz, LANES, t_pad), F32),
        scratch_shapes=[pltpu.VMEM((t_pad, LANES), F32)],
        compiler_params=_params("parallel"),
        name="fcum",
    )(cache_logf, logf_new)


def _row_to_cols(frow):
    return jnp.broadcast_to(frow, (LANES, frow.shape[1])).T


def _attn_prompt_kernel(q_ref, k_ref, v_ref, f_ref, o_ref, vt_scr, fk_scr, *, tb, hd, scale):
    T = q_ref.shape[0]
    nh = q_ref.shape[1] // hd
    nb = T // tb
    h0 = (pl.program_id(1) % (8 // nh)) * nh
    scale = scale * LOG2E
    frows = [f_ref[0, pl.ds(h0 + i, 1), :] * LOG2E for i in range(nh)]
    wide = 2 * tb
    for i in range(nh):
        fk_scr[i] = _row_to_cols(frows[i])
        for j in range(T // wide):
            vt_scr[i, j] = v_ref[j * wide:(j + 1) * wide, i * hd:(i + 1) * hd].astype(F32).T.astype(BF16)
    kofs = lax.broadcasted_iota(jnp.int32, (tb, tb), 0)
    qofs = lax.broadcasted_iota(jnp.int32, (tb, tb), 1)

    for qi in range(nb):
        qs = slice(qi * tb, (qi + 1) * tb)
        q_t = [q_ref[qs, i * hd:(i + 1) * hd].astype(F32).T.astype(BF16) for i in range(nh)]

        def block(start, width, vt, carry, masked):
            qk = [jnp.dot(k_ref[pl.ds(start, width), i * hd:(i + 1) * hd], q_t[i], preferred_element_type=F32)
                  for i in range(nh)]
            out = []
            for i in range(nh):
                m, l, acc = carry[3 * i:3 * i + 3]
                fk = fk_scr[i, pl.ds(start, width), :]
                s = qk[i] * scale
                s = s + frows[i][:, qs] - jnp.concatenate([fk] * (tb // LANES), axis=1)
                if masked:
                    s = jnp.where(kofs <= qofs, s, -jnp.inf)
                m_new = jnp.maximum(m, jnp.max(s, axis=0, keepdims=True))
                alpha = jnp.exp2(m - m_new)
                p = jnp.exp2(s - m_new)
                l = alpha * l + jnp.sum(p, axis=0, keepdims=True)
                acc = alpha * acc + jnp.dot(vt(i), p.astype(BF16), preferred_element_type=F32)
                out += [m_new, l, acc]
            return tuple(out)

        def wide_block(jw, carry):
            return block(pl.multiple_of(jw * wide, wide), wide, lambda i: vt_scr[i, jw], carry, False)

        def narrow_block(j, carry, masked):
            half = slice((j % 2) * tb, (j % 2 + 1) * tb)
            return block(j * tb, tb, lambda i: vt_scr[i, j // 2, :, half], carry, masked)

        carry = (jnp.full((1, tb), -jnp.inf, F32), jnp.zeros((1, tb), F32), jnp.zeros((hd, tb), F32)) * nh
        if qi // 2:
            carry = lax.fori_loop(0, qi // 2, wide_block, carry)
        if qi % 2:
            carry = narrow_block(qi - 1, carry, False)
        carry = narrow_block(qi, carry, True)
        for i in range(nh):
            o_ref[qs, i * hd:(i + 1) * hd] = (carry[3 * i + 2] / carry[3 * i + 1]).T.astype(o_ref.dtype)


def attn_prompt(q, k, v, f_t, *, Bsz, T, heads, hd):
    tb = _tile(T, ATTN_BLOCK)
    nh = ATTN_HEADS_PER_STEP
    assert tb % LANES == 0 and T % (2 * tb) == 0 and heads % 8 == 0 and 8 % nh == 0 and hd == LANES
    blk = pl.BlockSpec((T, nh * hd), lambda b, h: (b, h))
    return pl.pallas_call(
        functools.partial(_attn_prompt_kernel, tb=tb, hd=hd, scale=hd ** -0.5),
        grid=(Bsz, heads // nh),
        in_specs=[blk, blk, blk, pl.BlockSpec((1, 8, T), lambda b, h: (b, h * nh // 8, 0))],
        out_specs=blk,
        out_shape=jax.ShapeDtypeStruct((Bsz * T, heads * hd), BF16),
        scratch_shapes=[pltpu.VMEM((nh, T // (2 * tb), hd, 2 * tb), BF16), pltpu.VMEM((nh, T, LANES), F32)],
        compiler_params=_params("parallel", "parallel"),
        name="attn_prompt",
    )(q, k, v, f_t)


def _attn_sample_kernel(q_ref, kc_ref, vc_ref, kn_ref, vn_ref, f_ref, o_ref, *, Tn, past, hd, nh, scale):
    zrows = jnp.zeros((LANES - Tn, hd), BF16)
    rowi = lax.broadcasted_iota(jnp.int32, (Tn, LANES), 0)
    coli = lax.broadcasted_iota(jnp.int32, (Tn, LANES), 1)
    nt = (((1,), (1,)), ((), ()))
    for hh in range(nh):
        sl = slice(hh * hd, (hh + 1) * hd)
        q = q_ref[:, sl]
        kc = kc_ref[0, :, sl].astype(BF16)
        vc = vc_ref[0, :, sl].astype(BF16)
        kn = jnp.concatenate([kn_ref[:, sl], zrows], axis=0)
        vn = jnp.concatenate([vn_ref[:, sl], zrows], axis=0)
        f_past = f_ref[0, hh:hh + 1, 0:past]
        f_new = f_ref[0, hh:hh + 1, past:past + LANES]
        fq_col = _row_to_cols(f_new)[0:Tn, 0:1]
        s1 = lax.dot_general(q, kc, nt, preferred_element_type=F32) * scale + fq_col - f_past
        s2 = lax.dot_general(q, kn, nt, preferred_element_type=F32) * scale + fq_col - f_new
        s2 = jnp.where(coli <= rowi, s2, -jnp.inf)
        m = jnp.maximum(jnp.max(s1, axis=1, keepdims=True), jnp.max(s2, axis=1, keepdims=True))
        p1 = jnp.exp(s1 - m)
        p2 = jnp.exp(s2 - m)
        l = jnp.sum(p1, axis=1, keepdims=True) + jnp.sum(p2, axis=1, keepdims=True)
        o = jnp.dot(p1.astype(BF16), vc, preferred_element_type=F32)
        o = o + jnp.dot(p2.astype(BF16), vn, preferred_element_type=F32)
        o_ref[:, sl] = (o / l).astype(o_ref.dtype)


def attn_sample(q, cache_k, cache_v, k_new, v_new, f_t, *, Bsz, Tn, heads, hd):
    past = cache_k.shape[1]
    nh = 8
    assert past % LANES == 0 and Tn % 16 == 0 and Tn <= LANES and heads % nh == 0
    ck = cache_k.reshape(Bsz, past, heads * hd)
    cv = cache_v.reshape(Bsz, past, heads * hd)
    return pl.pallas_call(
        functools.partial(_attn_sample_kernel, Tn=Tn, past=past, hd=hd, nh=nh, scale=hd ** -0.5),
        grid=(Bsz, heads // nh),
        in_specs=[pl.BlockSpec((Tn, nh * hd), lambda b, g: (b, g)),
                  pl.BlockSpec((1, past, nh * hd), lambda b, g: (b, 0, g)),
                  pl.BlockSpec((1, past, nh * hd), lambda b, g: (b, 0, g)),
                  pl.BlockSpec((Tn, nh * hd), lambda b, g: (b, g)),
                  pl.BlockSpec((Tn, nh * hd), lambda b, g: (b, g)),
                  pl.BlockSpec((1, nh, f_t.shape[2]), lambda b, g: (b, g, 0))],
        out_specs=pl.BlockSpec((Tn, nh * hd), lambda b, g: (b, g)),
        out_shape=jax.ShapeDtypeStruct((Bsz * Tn, heads * hd), BF16),
        compiler_params=_params("parallel", "parallel"),
        name="attn_sample",
    )(q, ck, cv, k_new, v_new, f_t)


def _trunk(x, conv_st, ssm_st, cache_k, cache_v, cache_logf, w, p):
    Bsz, T, D = x.shape
    M = Bsz * T
    heads, hd = cache_k.shape[2], cache_k.shape[3]
    d_inner = p["a_norm"].shape[-1]
    d_ff = p["d_ff"]
    n_state = ssm_st.shape[-1]
    groups = (conv_st.shape[-1] - d_inner) // (2 * n_state)
    d_zxbc = 2 * d_inner + 2 * groups * n_state
    L = SSD_CHUNK if T % SSD_CHUNK == 0 else T
    small = M <= 512
    wb = {}

    norm_ffn = p["norm_ffn"].reshape(-1, D)
    tn_r = 256 if small else 512

    def ffn(h, xn, idx, next_gains):
        a, wb[f"wg{idx}"], wb[f"wu{idx}"] = mm_gateup(xn, w[f"wg{idx}"], w[f"wu{idx}"], n_cols=d_ff)
        h, xns, wb[f"wd{idx}"] = mm_resid(a, w[f"wd{idx}"], h, 0.5, next_gains, tn=tn_r, name="mm_down")
        return h, xns

    h = x.reshape(M, D)
    h, (u,) = ffn(h, norm_prep(h, norm_ffn[0]), 0, [p["norm_mix"][0]])
    (proj,), wb["w_in"] = mm_plain(u, w["w_in"], [F32], n_cols=d_zxbc, tn=512 if small else 1024,
                                   name="mm_in_proj")
    dt, acum, wb["w_dt"] = mm_dt(u, w["w_dt"], p["a_dt_bias"][0], p["a_A_log"][0], L=L)
    y, new_conv, new_ssm = ssd_mixer(
        proj, dt, acum, conv_st[0], ssm_st[0], p["a_conv_w"][0], p["a_conv_b"][0], p["a_D"][0], p["a_norm"][0],
        Bsz=Bsz, T=T, L=L, d_inner=d_inner, groups=groups, n_state=n_state)
    h, (xn,), wb["w_out"] = mm_resid(y, w["w_out"], h, 1.0, [norm_ffn[1]], tm=1024, tn=256, name="mm_out_proj")
    h, (x_kv, xn) = ffn(h, xn, 1, [p["kv_norm"], norm_ffn[2]])
    tn = 512 if small else 1024
    (k32, kb), wb["w_k"] = mm_plain(x_kv, w["w_k"], [F32, BF16], n_cols=D, tn=tn, name="mm_k")
    (v32, vb), wb["w_v"] = mm_plain(x_kv, w["w_v"], [F32, BF16], n_cols=D, tn=tn, name="mm_v")
    logf = mm_logf(x_kv, p["w_f"], p["b_f"])
    f_t = fcum(cache_logf, logf, Bsz=Bsz, Tn=T)
    h, (u,) = ffn(h, xn, 2, [p["norm_mix"][1]])
    (q,), wb["w_q"] = mm_plain(u, w["w_q"], [BF16], n_cols=D, tn=tn, name="mm_q")
    if cache_k.shape[1] == 0:
        o = attn_prompt(q, kb, vb, f_t, Bsz=Bsz, T=T, heads=heads, hd=hd)
    else:
        o = attn_sample(q, cache_k, cache_v, kb, vb, f_t, Bsz=Bsz, Tn=T, heads=heads, hd=hd)
    h, (xn,), wb["w_o"] = mm_resid(o, w["w_o"], h, 1.0, [norm_ffn[3]], tm=1024, tn=512, name="mm_wo")
    h, _ = ffn(h, xn, 3, [])
    y_out = rmsnorm(h, p["norm_final"], F32).reshape(Bsz, T, D)
    outs = (y_out, new_conv[None], new_ssm[None],
            k32.reshape(Bsz, T, heads, hd), v32.reshape(Bsz, T, heads, hd),
            logf[:, :heads].reshape(Bsz, T, heads))
    return outs, wb


def kernel(x_prompt, x_sample, state_conv, state_ssm, cache_k, cache_v, cache_logf, norm_ffn, ffn_wg, ffn_wu,
           ffn_wd, norm_mix, a_w_in, a_conv_w, a_conv_b, a_dt_bias, a_A_log, a_D, a_norm, a_w_out, kv_norm,
           w_kvf, b_f, b_wq, b_wo, norm_final):
    assert norm_mix.shape[0] == 2 and a_w_in.shape[0] == 1 and b_wq.shape[0] == 1
    D = x_prompt.shape[-1]
    F = ffn_wg.shape[-1]
    heads = cache_k.shape[2]
    d_zxbc = a_w_in.shape[-1] - a_dt_bias.shape[-1]
    p = {"norm_ffn": norm_ffn, "norm_mix": norm_mix, "a_conv_w": a_conv_w, "a_conv_b": a_conv_b,
         "a_dt_bias": a_dt_bias, "a_A_log": a_A_log, "a_D": a_D, "a_norm": a_norm, "kv_norm": kv_norm,
         "norm_final": norm_final, "d_ff": F,
         "w_f": jnp.pad(w_kvf[:, 2 * D:], ((0, 0), (0, LANES - heads))).astype(BF16),
         "b_f": jnp.pad(b_f, (0, LANES - heads)).reshape(1, LANES)}
    wg, wu, wd = ffn_wg.reshape(-1, F), ffn_wu.reshape(-1, F), ffn_wd.reshape(-1, D)
    w = {"w_in": (a_w_in[0], 0, 0), "w_dt": (a_w_in[0], 0, d_zxbc), "w_out": (a_w_out[0], 0, 0),
         "w_k": (w_kvf, 0, 0), "w_v": (w_kvf, 0, D), "w_q": (b_wq[0], 0, 0), "w_o": (b_wo[0], 0, 0)}
    for idx in range(wg.shape[0] // D):
        w[f"wg{idx}"], w[f"wu{idx}"], w[f"wd{idx}"] = (wg, idx, 0), (wu, idx, 0), (wd, idx, 0)

    bp = x_prompt.shape[0]
    zero_conv = jnp.zeros((state_conv.shape[0], bp) + state_conv.shape[2:], state_conv.dtype)
    zero_ssm = jnp.zeros((state_ssm.shape[0], bp) + state_ssm.shape[2:], state_ssm.dtype)
    empty_kv = jnp.zeros((bp, 0) + cache_k.shape[2:], cache_k.dtype)
    empty_lf = jnp.zeros((bp, 0, heads), cache_logf.dtype)
    (y_s, s_conv, s_ssm, s_k, s_v, s_logf), wb = _trunk(
        x_sample, state_conv, state_ssm, cache_k, cache_v, cache_logf, w, p)
    (y_p, p_conv, p_ssm, p_k, p_v, p_logf), _ = _trunk(
        x_prompt, zero_conv, zero_ssm, empty_kv, empty_kv, empty_lf, wb, p)
    return (y_p, y_s, p_conv, p_ssm, p_k, p_v, p_logf, s_conv, s_ssm, s_k, s_v, s_logf)
```

```python
import functools

import numpy as np
import jax
import jax.numpy as jnp
from jax import lax
from jax.experimental import pallas as pl
from jax.experimental.pallas import tpu as pltpu

NORM_EPS = 1e-5
SSD_CHUNK = 64
SSM_HEAD_DIM = 64
CONV_W = 4
LANES = 128
LOG2E = 1.4426950408889634
VMEM_LIMIT_BYTES = 56 * 1024 * 1024
CONV_VMEM_LIMIT_BYTES = 60 * 1024 * 1024
SSD_GROUPS_PER_STEP = 4
GATEUP_ROW_CHUNK = 512
CONV_COL_CHUNK = 256
ATTN_BLOCK = 256
ATTN_HEADS_PER_STEP = 4
BF16 = jnp.bfloat16
F32 = jnp.float32


def _params(*sem, vmem=VMEM_LIMIT_BYTES):
    return pltpu.CompilerParams(dimension_semantics=sem, vmem_limit_bytes=vmem)


def _tile(n, pref):
    if n <= pref:
        return n
    t = 1 << (pref.bit_length() - 1)
    while n % t:
        t //= 2
    return t


def _sigmoid(x):
    return 1.0 / (1.0 + jnp.exp(-x))


def _silu(x):
    return x * _sigmoid(x)


def _softplus(x):
    return jnp.maximum(x, 0.0) + jnp.log1p(jnp.exp(-jnp.abs(x)))


def _cumsum_rows(x, seg):
    rowi = jnp.bitwise_and(lax.broadcasted_iota(jnp.int32, x.shape, 0), seg - 1)
    s = 1
    while s < seg:
        x = x + jnp.where(rowi >= s, pltpu.roll(x, s, 0), 0.0)
        s *= 2
    return x


def _rmsnorm_kernel(x_ref, g_ref, o_ref):
    x = x_ref[...]
    y = x * lax.rsqrt(jnp.mean(x * x, axis=-1, keepdims=True) + NORM_EPS)
    o_ref[...] = (y * g_ref[...]).astype(o_ref.dtype)


def rmsnorm(x, g, out_dtype):
    M, D = x.shape
    tm = _tile(M, 512)
    return pl.pallas_call(
        _rmsnorm_kernel,
        grid=(M // tm,),
        in_specs=[pl.BlockSpec((tm, D), lambda i: (i, 0)),
                  pl.BlockSpec((1, D), lambda i: (0, 0))],
        out_specs=pl.BlockSpec((tm, D), lambda i: (i, 0)),
        out_shape=jax.ShapeDtypeStruct((M, D), out_dtype),
        compiler_params=_params("parallel"),
        name="rmsnorm",
    )(x, g.reshape(1, D))


def _fold_lanes(v):
    parts = [v[:, k * LANES:(k + 1) * LANES] for k in range(v.shape[1] // LANES)]
    while len(parts) > 1:
        parts = [a + b for a, b in zip(parts[::2], parts[1::2])] + parts[len(parts) & ~1:]
    return parts[0]


def _row_scale(ssq, d):
    r = lax.rsqrt(jnp.sum(ssq, axis=-1, keepdims=True) * (1.0 / d) + NORM_EPS)
    return jnp.broadcast_to(r, ssq.shape)


def _scale_rows(acc, rs_ref):
    return acc * jnp.concatenate([rs_ref[...]] * (acc.shape[1] // LANES), axis=1)


def _norm_prep_kernel(x_ref, g_ref, hb_ref, rs_ref):
    x = x_ref[...]
    hb_ref[...] = (x * g_ref[...]).astype(hb_ref.dtype)
    rs_ref[...] = _row_scale(_fold_lanes(x * x), x.shape[1])


def norm_prep(x, g):
    M, D = x.shape
    tm = _tile(M, 512)
    hb, rs = pl.pallas_call(
        _norm_prep_kernel,
        grid=(M // tm,),
        in_specs=[pl.BlockSpec((tm, D), lambda i: (i, 0)),
                  pl.BlockSpec((1, D), lambda i: (0, 0))],
        out_specs=[pl.BlockSpec((tm, D), lambda i: (i, 0)),
                   pl.BlockSpec((tm, LANES), lambda i: (i, 0))],
        out_shape=[jax.ShapeDtypeStruct((M, D), BF16), jax.ShapeDtypeStruct((M, LANES), F32)],
        compiler_params=_params("parallel"),
        name="norm_prep",
    )(x, g.reshape(1, D))
    return hb, rs


def _w_bf16(w_ref, wb_ref):
    w = w_ref[...]
    if wb_ref is not None:
        w = w.astype(BF16)
        wb_ref[...] = w
    return w


def _w_specs(w, K, n_cols, tn, M, tm):
    arr, row_blk, col_off = w
    emit = arr.dtype != BF16
    assert col_off % tn == 0 and n_cols % tn == 0
    assert not emit or M == tm
    cb = col_off // tn
    in_spec = pl.BlockSpec((K, tn), lambda i, j: (row_blk, cb + j))
    out_specs = [pl.BlockSpec((K, tn), lambda i, j: (0, j))] if emit else []
    out_shapes = [jax.ShapeDtypeStruct((K, n_cols), BF16)] if emit else []
    return arr, emit, in_spec, out_specs, out_shapes


def _x_specs(xn, tm):
    hb, _ = xn
    return [pl.BlockSpec((tm, hb.shape[1]), lambda i, j: (i, 0)),
            pl.BlockSpec((tm, LANES), lambda i, j: (i, 0))]


def _mm_plain_kernel(x_ref, rs_ref, w_ref, *o_refs, emit, silu):
    wb_ref = o_refs[-1] if emit else None
    acc = jnp.dot(x_ref[...], _w_bf16(w_ref, wb_ref), preferred_element_type=F32)
    acc = _scale_rows(acc, rs_ref)
    if silu:
        acc = _silu(acc)
    for o_ref in o_refs[:len(o_refs) - emit]:
        o_ref[...] = acc.astype(o_ref.dtype)


def mm_plain(xn, w, out_dtypes, *, n_cols, tm=1024, tn=1024, silu=False, name="mm_plain"):
    M, K = xn[0].shape
    tm, tn = _tile(M, tm), _tile(n_cols, tn)
    arr, emit, w_spec, wb_specs, wb_shapes = _w_specs(w, K, n_cols, tn, M, tm)
    outs = pl.pallas_call(
        functools.partial(_mm_plain_kernel, emit=emit, silu=silu),
        grid=(M // tm, n_cols // tn),
        in_specs=_x_specs(xn, tm) + [w_spec],
        out_specs=[pl.BlockSpec((tm, tn), lambda i, j: (i, j)) for _ in out_dtypes] + wb_specs,
        out_shape=[jax.ShapeDtypeStruct((M, n_cols), dt) for dt in out_dtypes] + wb_shapes,
        compiler_params=_params("parallel", "arbitrary"),
        name=name,
    )(*xn, arr)
    n = len(out_dtypes)
    return outs[:n], ((outs[n], 0, 0) if emit else w)


def _mm_conv_kernel(x_ref, rs_ref, w_ref, cw_ref, cb_ref, st_ref, o_ref, tail_ref, *rest,
                    emit, seg_len, tiles_per_seq):
    pad_scr, carry_scr = rest[-2:]
    i, j = pl.program_id(0), pl.program_id(1)
    tm, tn = o_ref.shape
    if emit:
        rest[0][...] = w_ref[...].astype(BF16)
        w_ref = rest[0]
    tail_ref[...] = jnp.zeros_like(tail_ref)

    def conv_silu(rows, cols):
        acc = cb_ref[:, cols] + pad_scr[5:5 + rows, cols] * cw_ref[0:1, cols]
        for k in range(1, CONV_W):
            acc = acc + pad_scr[5 + k:5 + k + rows, cols] * cw_ref[k:k + 1, cols]
        return _silu(acc)

    if seg_len == tm:
        prev = st_ref[0]
        if tiles_per_seq > 1:
            @pl.when(i == 0)
            def _():
                carry_scr[j] = jnp.zeros(carry_scr.shape[1:], F32)

            prev = jnp.where(i % tiles_per_seq == 0, prev, carry_scr[j, 0:CONV_W - 1, :])
        pad_scr[5:8, :] = prev
        x = x_ref[...]
        rs = jnp.concatenate([rs_ref[...]] * (CONV_COL_CHUNK // LANES), axis=1)
        for c0 in range(0, tn, CONV_COL_CHUNK):
            cols = slice(c0, c0 + CONV_COL_CHUNK)
            pad_scr[8:8 + tm, cols] = jnp.dot(x, w_ref[:, cols], preferred_element_type=F32) * rs
            o_ref[:, cols] = conv_silu(tm, cols)
        last = pad_scr[5 + tm:8 + tm, :]
        tail_ref[0, 0:CONV_W - 1, :] = last
        if tiles_per_seq > 1:
            carry_scr[j, 0:CONV_W - 1, :] = last
    else:
        raw = _scale_rows(jnp.dot(x_ref[...], w_ref[...], preferred_element_type=F32), rs_ref)
        for s in range(tm // seg_len):
            pad_scr[5:8, :] = st_ref[s]
            pad_scr[8:8 + seg_len, :] = raw[s * seg_len:(s + 1) * seg_len]
            o_ref[s * seg_len:(s + 1) * seg_len, :] = conv_silu(seg_len, slice(0, tn))
            tail_ref[s, 0:CONV_W - 1, :] = pad_scr[5 + seg_len:8 + seg_len, :]


def mm_conv(xn, w, conv_w, conv_b, conv_state, *, T, n_cols, tm=1024, tn=1024):
    M, K = xn[0].shape
    Bsz = M // T
    tm, tn = _tile(M, tm), _tile(n_cols, tn)
    seg_len = min(T, tm)
    assert tm % seg_len == 0 and T % seg_len == 0 and seg_len >= CONV_W - 1
    nseg, tiles_per_seq = tm // seg_len, T // seg_len
    arr, emit, w_spec, wb_specs, wb_shapes = _w_specs(w, K, n_cols, tn, M, tm)
    outs = pl.pallas_call(
        functools.partial(_mm_conv_kernel, emit=emit, seg_len=seg_len, tiles_per_seq=tiles_per_seq),
        grid=(M // tm, n_cols // tn),
        in_specs=_x_specs(xn, tm) + [
            w_spec,
            pl.BlockSpec((CONV_W, tn), lambda i, j: (0, j)),
            pl.BlockSpec((1, tn), lambda i, j: (0, j)),
            pl.BlockSpec((nseg, CONV_W - 1, tn), lambda i, j: (i // tiles_per_seq, 0, j))],
        out_specs=[pl.BlockSpec((tm, tn), lambda i, j: (i, j)),
                   pl.BlockSpec((nseg, 8, tn), lambda i, j: (i, 0, j))] + wb_specs,
        out_shape=[jax.ShapeDtypeStruct((M, n_cols), F32),
                   jax.ShapeDtypeStruct((M // seg_len, 8, n_cols), F32)] + wb_shapes,
        scratch_shapes=[pltpu.VMEM((8 + tm if nseg == 1 else 8 + seg_len, tn), F32),
                        pltpu.VMEM((n_cols // tn, 8, tn), F32)],
        compiler_params=_params("arbitrary", "arbitrary", vmem=CONV_VMEM_LIMIT_BYTES),
        name="mm_conv",
    )(*xn, arr, conv_w, conv_b.reshape(1, n_cols), conv_state)
    tails = outs[1].reshape(Bsz, tiles_per_seq, 8, n_cols)[:, -1, 0:CONV_W - 1, :]
    return outs[0], tails, ((outs[2], 0, 0) if emit else w)


def _mm_resid_kernel(*refs, alpha, n_norm, emit, n_total):
    x_ref, w_ref, r_ref = refs[:3]
    g_refs = refs[3:3 + n_norm]
    o_ref = refs[3 + n_norm]
    hb_refs = refs[4 + n_norm:4 + 2 * n_norm]
    rs_ref = refs[4 + 2 * n_norm] if n_norm else None
    wb_ref = refs[-1] if emit else None
    acc = jnp.dot(x_ref[...], _w_bf16(w_ref, wb_ref), preferred_element_type=F32)
    h = r_ref[...] + alpha * acc
    o_ref[...] = h
    for g_ref, hb_ref in zip(g_refs, hb_refs):
        hb_ref[...] = (h * g_ref[...]).astype(hb_ref.dtype)
    if n_norm:
        part = _fold_lanes(h * h)
        j = pl.program_id(1)

        @pl.when(j == 0)
        def _():
            rs_ref[...] = part

        @pl.when(j > 0)
        def _():
            rs_ref[...] += part

        @pl.when(j == pl.num_programs(1) - 1)
        def _():
            rs_ref[...] = _row_scale(rs_ref[...], n_total)


def mm_resid(x, w, resid, alpha, gains=(), *, tm=512, tn=512, name="mm_resid"):
    M, K = x.shape
    N = resid.shape[1]
    tm, tn = _tile(M, tm), _tile(N, tn)
    n_norm = len(gains)
    arr, emit, w_spec, wb_specs, wb_shapes = _w_specs(w, K, N, tn, M, tm)
    tile = pl.BlockSpec((tm, tn), lambda i, j: (i, j))
    outs = pl.pallas_call(
        functools.partial(_mm_resid_kernel, alpha=alpha, n_norm=n_norm, emit=emit, n_total=N),
        grid=(M // tm, N // tn),
        in_specs=[pl.BlockSpec((tm, K), lambda i, j: (i, 0)), w_spec, tile]
                 + [pl.BlockSpec((1, tn), lambda i, j: (0, j))] * n_norm,
        out_specs=[tile] * (1 + n_norm) + [pl.BlockSpec((tm, LANES), lambda i, j: (i, 0))] * bool(n_norm) + wb_specs,
        out_shape=[jax.ShapeDtypeStruct((M, N), F32)] + [jax.ShapeDtypeStruct((M, N), BF16)] * n_norm
                  + [jax.ShapeDtypeStruct((M, LANES), F32)] * bool(n_norm) + wb_shapes,
        compiler_params=_params("parallel", "arbitrary"),
        name=name,
    )(x, arr, resid, *[g.reshape(1, N) for g in gains])
    xns = [(outs[1 + k], outs[1 + n_norm]) for k in range(n_norm)]
    return outs[0], xns, ((outs[-1], 0, 0) if emit else w)


def _gateup_epilogue(g, u, rs_ref, o_ref):
    o_ref[...] = (_silu(_scale_rows(g, rs_ref)) * _scale_rows(u, rs_ref)).astype(o_ref.dtype)


def _mm_gateup_pair_kernel(x_ref, rs_ref, wg_ref, wu_ref, o_ref, *wgu_refs):
    x = x_ref[...]
    wg, wu = wg_ref[...], wu_ref[...]
    if wgu_refs:
        tn = wg.shape[1]
        wg, wu = wg.astype(BF16), wu.astype(BF16)
        wgu_refs[0][:, 0:tn] = wg
        wgu_refs[0][:, tn:2 * tn] = wu
    g = jnp.dot(x, wg, preferred_element_type=F32)
    u = jnp.dot(x, wu, preferred_element_type=F32)
    _gateup_epilogue(g, u, rs_ref, o_ref)


def _mm_gateup_kernel(x_ref, rs_ref, wgu_ref, o_ref):
    tm, tn = o_ref.shape
    rows = min(tm, GATEUP_ROW_CHUNK)
    w = wgu_ref[...]
    for r0 in range(0, tm, rows):
        rsl = slice(r0, r0 + rows)
        gu = jnp.dot(x_ref[rsl, :], w, preferred_element_type=F32)
        rs = jnp.concatenate([rs_ref[rsl, :]] * (tn // LANES), axis=1)
        o_ref[rsl, :] = (_silu(gu[:, 0:tn] * rs) * (gu[:, tn:2 * tn] * rs)).astype(o_ref.dtype)


def mm_gateup(xn, wgu, *, n_cols, tm=2048, tn=256):
    M, K = xn[0].shape
    tm, tn = _tile(M, tm), _tile(n_cols, tn)
    out_spec = pl.BlockSpec((tm, tn), lambda i, j: (i, j))
    out_shape = jax.ShapeDtypeStruct((M, n_cols), BF16)
    if isinstance(wgu[0], tuple):
        garr, emit, g_spec, _, _ = _w_specs(wgu[0], K, n_cols, tn, M, tm)
        uarr, _, u_spec, _, _ = _w_specs(wgu[1], K, n_cols, tn, M, tm)
        outs = pl.pallas_call(
            _mm_gateup_pair_kernel,
            grid=(M // tm, n_cols // tn),
            in_specs=_x_specs(xn, tm) + [g_spec, u_spec],
            out_specs=[out_spec] + [pl.BlockSpec((K, 2 * tn), lambda i, j: (0, j))] * emit,
            out_shape=[out_shape] + [jax.ShapeDtypeStruct((K, 2 * n_cols), BF16)] * emit,
            compiler_params=_params("parallel", "arbitrary"),
            name="mm_gateup",
        )(*xn, garr, uarr)
        return outs[0], ((outs[1], 0, 0) if emit else wgu)
    arr, row_blk, col_off = wgu
    assert row_blk == 0 and col_off == 0 and arr.shape == (K, 2 * n_cols)
    out = pl.pallas_call(
        _mm_gateup_kernel,
        grid=(M // tm, n_cols // tn),
        in_specs=_x_specs(xn, tm) + [pl.BlockSpec((K, 2 * tn), lambda i, j: (0, j))],
        out_specs=out_spec,
        out_shape=out_shape,
        compiler_params=_params("parallel", "arbitrary"),
        name="mm_gateup",
    )(*xn, arr)
    return out, wgu


def _mm_dt_kernel(x_ref, rs_ref, w_ref, b_ref, alog_ref, dt_ref, acum_ref, *wb_refs, L):
    raw = jnp.dot(x_ref[...], _w_bf16(w_ref, wb_refs[0] if wb_refs else None), preferred_element_type=F32)
    dt = _softplus(_scale_rows(raw, rs_ref) + b_ref[...])
    dt_ref[...] = dt
    acum_ref[...] = _cumsum_rows(dt * (-jnp.exp(alog_ref[...])), L)


def mm_dt(xn, w, dt_bias, a_log, *, L, tm=2048):
    M, K = xn[0].shape
    heads = dt_bias.shape[0]
    tm = _tile(M, tm)
    assert tm % L == 0 and L & (L - 1) == 0
    arr, emit, w_spec, wb_specs, wb_shapes = _w_specs(w, K, heads, heads, M, tm)
    vec = pl.BlockSpec((1, heads), lambda i, j: (0, 0))
    outs = pl.pallas_call(
        functools.partial(_mm_dt_kernel, L=L),
        grid=(M // tm, 1),
        in_specs=_x_specs(xn, tm) + [w_spec, vec, vec],
        out_specs=[pl.BlockSpec((tm, heads), lambda i, j: (i, 0))] * 2 + wb_specs,
        out_shape=[jax.ShapeDtypeStruct((M, heads), F32)] * 2 + wb_shapes,
        compiler_params=_params("parallel", "arbitrary"),
        name="mm_dt",
    )(*xn, arr, dt_bias.reshape(1, heads), a_log.reshape(1, heads))
    return outs[0], outs[1], ((outs[2], 0, 0) if emit else w)


def _mm_logf_kernel(x_ref, rs_ref, w_ref, b_ref, o_ref):
    s = jnp.dot(x_ref[...], w_ref[...], preferred_element_type=F32)
    s = _scale_rows(s, rs_ref) + b_ref[...]
    o_ref[...] = -_softplus(-s)


def mm_logf(xn, w, b):
    M, K = xn[0].shape
    N = w.shape[1]
    tm = _tile(M, 2048)
    return pl.pallas_call(
        _mm_logf_kernel,
        grid=(M // tm, 1),
        in_specs=_x_specs(xn, tm) + [pl.BlockSpec((K, N), lambda i, j: (0, 0)),
                                     pl.BlockSpec((1, N), lambda i, j: (0, 0))],
        out_specs=pl.BlockSpec((tm, N), lambda i, j: (i, 0)),
        out_shape=jax.ShapeDtypeStruct((M, N), F32),
        compiler_params=_params("parallel", "arbitrary"),
        name="mm_logf",
    )(*xn, w, b)


def _split3(v):
    hi = v.astype(BF16)
    r1 = v - hi.astype(F32)
    mid = r1.astype(BF16)
    lo = (r1 - mid.astype(F32)).astype(BF16)
    return [hi, mid, lo]


def _ssd_kernel(gate_ref, x_ref, b_ref, c_ref, dt_ref, acum_ref, dx_ref, ng_ref, e_ref, h0_ref,
                y_ref, hout_ref, ht_scr, *, L, GW):
    c = pl.program_id(2)
    P = SSM_HEAD_DIM
    N = LANES
    n_sub = x_ref.shape[1] // GW

    @pl.when(c == 0)
    def _():
        for s in range(n_sub):
            ht_scr[s] = h0_ref[0, s * GW:(s + 1) * GW, :].T

    lhs = jnp.concatenate(_split3(acum_ref[...]) + _split3(dt_ref[...]), axis=0)

    lane = lax.broadcasted_iota(jnp.int32, (L, LANES), 1)
    rowi = lax.broadcasted_iota(jnp.int32, (L, LANES), 0)
    s_idx = jnp.bitwise_and(lane, P - 1)
    r_idx = jnp.right_shift(lane, P.bit_length() - 1)
    eye2 = rowi == s_idx
    tri2 = s_idx <= rowi
    pad = [] if L == P else [jnp.zeros((P - L, LANES), BF16)]

    for s in range(n_sub):
        xs = x_ref[:, s * GW:(s + 1) * GW]
        Bb = b_ref[:, s * N:(s + 1) * N].astype(BF16)
        Cb = c_ref[:, s * N:(s + 1) * N].astype(BF16)
        ex = jnp.dot(lhs, e_ref[s], preferred_element_type=F32)
        acum_x = ex[0:L] + ex[L:2 * L] + ex[2 * L:3 * L]
        dt_x = ex[3 * L:4 * L] + ex[4 * L:5 * L] + ex[5 * L:6 * L]

        xdt = xs * dt_x
        ht = ht_scr[s]
        y_inter = jnp.dot(Cb, ht.astype(BF16), preferred_element_type=F32) * jnp.exp(acum_x)

        b2 = jnp.concatenate([Bb] + pad + [Bb] + pad, axis=0)
        cb2 = lax.dot_general(Cb, b2, (((1,), (1,)), ((), ())), preferred_element_type=F32)
        y_parts = []
        for j in range(GW // LANES):
            sl = slice(j * LANES, (j + 1) * LANES)
            ax = acum_x[:, sl]
            v = jnp.sum(jnp.where(eye2, ax, 0.0), axis=0, keepdims=True)
            decay = jnp.exp(jnp.where(tri2, ax - v, -jnp.inf))
            m2 = (cb2 * decay).astype(BF16)
            xp = xdt[:, sl]
            top = jnp.where(r_idx == 0, xp, 0.0).astype(BF16)
            bot = jnp.where(r_idx == 1, xp, 0.0).astype(BF16)
            bd = jnp.concatenate([top] + pad + [bot] + pad, axis=0)
            y_parts.append(jnp.dot(m2, bd, preferred_element_type=F32))
        y = jnp.concatenate(y_parts, axis=1) + y_inter + dx_ref[:, s * GW:(s + 1) * GW] * xs

        y = y * gate_ref[:, s * GW:(s + 1) * GW]
        y = y * lax.rsqrt(jnp.mean(y * y, axis=-1, keepdims=True) + NORM_EPS)
        y_ref[:, s * GW:(s + 1) * GW] = (y * ng_ref[:, s * GW:(s + 1) * GW]).astype(y_ref.dtype)

        last = acum_x[L - 1:L, :]
        xt = (xdt * jnp.exp(last - acum_x)).astype(BF16)
        upd = lax.dot_general(Bb, xt, (((0,), (0,)), ((), ())), preferred_element_type=F32)
        ht_new = ht * jnp.exp(last) + upd
        ht_scr[s] = ht_new

        @pl.when(c == pl.num_programs(2) - 1)
        def _():
            hout_ref[0, s * GW:(s + 1) * GW, :] = ht_new.T


def ssd_mixer(gate, xbc, dt, acum, ssm_state, d_skip, norm_g, *, Bsz, T, L, d_inner, groups, n_state):
    M = Bsz * T
    heads = dt.shape[1]
    P = SSM_HEAD_DIM
    GW = d_inner // groups
    N = n_state
    R = GW // P
    nc = T // L
    ns = SSD_GROUPS_PER_STEP
    assert N == LANES and heads == LANES and GW % LANES == 0 and L % 16 == 0 and L <= P and groups % ns == 0
    SW, SN = ns * GW, ns * N
    kb, kc = d_inner // SN, (d_inner + groups * N) // SN

    expand = np.zeros((groups, heads, GW), np.float32)
    for g in range(groups):
        for r in range(R):
            expand[g, g * R + r, r * P:(r + 1) * P] = 1.0
    expand = jnp.asarray(expand, BF16)

    d_x = jnp.repeat(d_skip, P).reshape(1, d_inner)
    h0 = ssm_state.reshape(Bsz, heads * P, N)

    row = lambda b, g, c: b * nc + c
    in_specs = [
        pl.BlockSpec((L, SW), lambda b, g, c: (row(b, g, c), g)),
        pl.BlockSpec((L, SW), lambda b, g, c: (row(b, g, c), g)),
        pl.BlockSpec((L, SN), lambda b, g, c: (row(b, g, c), kb + g)),
        pl.BlockSpec((L, SN), lambda b, g, c: (row(b, g, c), kc + g)),
        pl.BlockSpec((L, heads), lambda b, g, c: (row(b, g, c), 0)),
        pl.BlockSpec((L, heads), lambda b, g, c: (row(b, g, c), 0)),
        pl.BlockSpec((1, SW), lambda b, g, c: (0, g)),
        pl.BlockSpec((1, SW), lambda b, g, c: (0, g)),
        pl.BlockSpec((ns, heads, GW), lambda b, g, c: (g, 0, 0)),
        pl.BlockSpec((1, SW, N), lambda b, g, c: (b, g, 0)),
    ]
    out_specs = [
        pl.BlockSpec((L, SW), lambda b, g, c: (row(b, g, c), g)),
        pl.BlockSpec((1, SW, N), lambda b, g, c: (b, g, 0)),
    ]
    out_shape = [
        jax.ShapeDtypeStruct((M, d_inner), BF16),
        jax.ShapeDtypeStruct((Bsz, heads * P, N), F32),
    ]
    y, h_new = pl.pallas_call(
        functools.partial(_ssd_kernel, L=L, GW=GW),
        grid=(Bsz, groups // ns, nc),
        in_specs=in_specs,
        out_specs=out_specs,
        out_shape=out_shape,
        scratch_shapes=[pltpu.VMEM((ns, N, GW), F32)],
        compiler_params=_params("parallel", "parallel", "arbitrary"),
        name="ssd_mixer",
    )(gate, xbc, xbc, xbc, dt, acum, d_x, norm_g.reshape(1, d_inner), expand, h0)
    return y, h_new.reshape(Bsz, heads, P, N)


def _fcum_kernel(past_ref, new_ref, o_ref, w_scr, *, past, Tn, heads):
    w_scr[...] = jnp.zeros_like(w_scr)
    if past:
        w_scr[0:past, 0:heads] = past_ref[0]
    w_scr[past:past + Tn, :] = new_ref[...]
    n = w_scr.shape[0]
    o_ref[0] = _cumsum_rows(w_scr[...], 1 << (n - 1).bit_length()).T


def fcum(cache_logf, logf_new, *, Bsz, Tn):
    past, heads = cache_logf.shape[1], cache_logf.shape[2]
    t_pad = -(-(past + Tn) // LANES) * LANES
    if past == 0:
        cache_logf = jnp.zeros((Bsz, 8, heads), F32)
    return pl.pallas_call(
        functools.partial(_fcum_kernel, past=past, Tn=Tn, heads=heads),
        grid=(Bsz,),
        in_specs=[pl.BlockSpec((1,) + cache_logf.shape[1:], lambda b: (b, 0, 0)),
                  pl.BlockSpec((Tn, LANES), lambda b: (b, 0))],
        out_specs=pl.BlockSpec((1, LANES, t_pad), lambda b: (b, 0, 0)),
        out_shape=jax.ShapeDtypeStruct((Bsz, LANES, t_pad), F32),
        scratch_shapes=[pltpu.VMEM((t_pad, LANES), F32)],
        compiler_params=_params("parallel"),
        name="fcum",
    )(cache_logf, logf_new)


def _row_to_cols(frow):
    return jnp.broadcast_to(frow, (LANES, frow.shape[1])).T


def _attn_prompt_kernel(q_ref, k_ref, v_ref, f_ref, o_ref, vt_scr, fk_scr, *, tb, hd, scale):
    T = q_ref.shape[0]
    nh = q_ref.shape[1] // hd
    nb = T // tb
    h0 = (pl.program_id(1) % (8 // nh)) * nh
    scale = scale * LOG2E
    frows = [f_ref[0, pl.ds(h0 + i, 1), :] * LOG2E for i in range(nh)]
    wide = 2 * tb
    for i in range(nh):
        fk_scr[i] = _row_to_cols(frows[i])
        for j in range(T // wide):
            vt_scr[i, j] = v_ref[j * wide:(j + 1) * wide, i * hd:(i + 1) * hd].astype(F32).T.astype(BF16)
    kofs = lax.broadcasted_iota(jnp.int32, (tb, tb), 0)
    qofs = lax.broadcasted_iota(jnp.int32, (tb, tb), 1)

    for qi in range(nb):
        qs = slice(qi * tb, (qi + 1) * tb)
        q_t = [q_ref[qs, i * hd:(i + 1) * hd].astype(F32).T.astype(BF16) for i in range(nh)]

        def block(start, width, vt, carry, masked):
            qk = [jnp.dot(k_ref[pl.ds(start, width), i * hd:(i + 1) * hd], q_t[i], preferred_element_type=F32)
                  for i in range(nh)]
            out = []
            for i in range(nh):
                m, l, acc = carry[3 * i:3 * i + 3]
                fk = fk_scr[i, pl.ds(start, width), :]
                s = qk[i] * scale
                s = s + frows[i][:, qs] - jnp.concatenate([fk] * (tb // LANES), axis=1)
                if masked:
                    s = jnp.where(kofs <= qofs, s, -jnp.inf)
                m_new = jnp.maximum(m, jnp.max(s, axis=0, keepdims=True))
                alpha = jnp.exp2(m - m_new)
                p = jnp.exp2(s - m_new)
                l = alpha * l + jnp.sum(p, axis=0, keepdims=True)
                acc = alpha * acc + jnp.dot(vt(i), p.astype(BF16), preferred_element_type=F32)
                out += [m_new, l, acc]
            return tuple(out)

        def wide_block(jw, carry):
            return block(pl.multiple_of(jw * wide, wide), wide, lambda i: vt_scr[i, jw], carry, False)

        def narrow_block(j, carry, masked):
            half = slice((j % 2) * tb, (j % 2 + 1) * tb)
            return block(j * tb, tb, lambda i: vt_scr[i, j // 2, :, half], carry, masked)

        carry = (jnp.full((1, tb), -jnp.inf, F32), jnp.zeros((1, tb), F32), jnp.zeros((hd, tb), F32)) * nh
        if qi // 2:
            carry = lax.fori_loop(0, qi // 2, wide_block, carry)
        if qi % 2:
            carry = narrow_block(qi - 1, carry, False)
        carry = narrow_block(qi, carry, True)
        for i in range(nh):
            o_ref[qs, i * hd:(i + 1) * hd] = (carry[3 * i + 2] / carry[3 * i + 1]).T.astype(o_ref.dtype)


def attn_prompt(q, k, v, f_t, *, Bsz, T, heads, hd):
    tb = _tile(T, ATTN_BLOCK)
    nh = ATTN_HEADS_PER_STEP
    assert tb % LANES == 0 and T % (2 * tb) == 0 and heads % 8 == 0 and 8 % nh == 0 and hd == LANES
    blk = pl.BlockSpec((T, nh * hd), lambda b, h: (b, h))
    return pl.pallas_call(
        functools.partial(_attn_prompt_kernel, tb=tb, hd=hd, scale=hd ** -0.5),
        grid=(Bsz, heads // nh),
        in_specs=[blk, blk, blk, pl.BlockSpec((1, 8, T), lambda b, h: (b, h * nh // 8, 0))],
        out_specs=blk,
        out_shape=jax.ShapeDtypeStruct((Bsz * T, heads * hd), BF16),
        scratch_shapes=[pltpu.VMEM((nh, T // (2 * tb), hd, 2 * tb), BF16), pltpu.VMEM((nh, T, LANES), F32)],
        compiler_params=_params("parallel", "parallel"),
        name="attn_prompt",
    )(q, k, v, f_t)


def _attn_sample_kernel(q_ref, kc_ref, vc_ref, kn_ref, vn_ref, f_ref, o_ref, *, Tn, past, hd, nh, scale):
    zrows = jnp.zeros((LANES - Tn, hd), BF16)
    rowi = lax.broadcasted_iota(jnp.int32, (Tn, LANES), 0)
    coli = lax.broadcasted_iota(jnp.int32, (Tn, LANES), 1)
    nt = (((1,), (1,)), ((), ()))
    for hh in range(nh):
        sl = slice(hh * hd, (hh + 1) * hd)
        q = q_ref[:, sl]
        kc = kc_ref[0, :, sl].astype(BF16)
        vc = vc_ref[0, :, sl].astype(BF16)
        kn = jnp.concatenate([kn_ref[:, sl], zrows], axis=0)
        vn = jnp.concatenate([vn_ref[:, sl], zrows], axis=0)
        f_past = f_ref[0, hh:hh + 1, 0:past]
        f_new = f_ref[0, hh:hh + 1, past:past + LANES]
        fq_col = _row_to_cols(f_new)[0:Tn, 0:1]
        s1 = lax.dot_general(q, kc, nt, preferred_element_type=F32) * scale + fq_col - f_past
        s2 = lax.dot_general(q, kn, nt, preferred_element_type=F32) * scale + fq_col - f_new
        s2 = jnp.where(coli <= rowi, s2, -jnp.inf)
        m = jnp.maximum(jnp.max(s1, axis=1, keepdims=True), jnp.max(s2, axis=1, keepdims=True))
        p1 = jnp.exp(s1 - m)
        p2 = jnp.exp(s2 - m)
        l = jnp.sum(p1, axis=1, keepdims=True) + jnp.sum(p2, axis=1, keepdims=True)
        o = jnp.dot(p1.astype(BF16), vc, preferred_element_type=F32)
        o = o + jnp.dot(p2.astype(BF16), vn, preferred_element_type=F32)
        o_ref[:, sl] = (o / l).astype(o_ref.dtype)


def attn_sample(q, cache_k, cache_v, k_new, v_new, f_t, *, Bsz, Tn, heads, hd):
    past = cache_k.shape[1]
    nh = 8
    assert past % LANES == 0 and Tn % 16 == 0 and Tn <= LANES and heads % nh == 0
    ck = cache_k.reshape(Bsz, past, heads * hd)
    cv = cache_v.reshape(Bsz, past, heads * hd)
    return pl.pallas_call(
        functools.partial(_attn_sample_kernel, Tn=Tn, past=past, hd=hd, nh=nh, scale=hd ** -0.5),
        grid=(Bsz, heads // nh),
        in_specs=[pl.BlockSpec((Tn, nh * hd), lambda b, g: (b, g)),
                  pl.BlockSpec((1, past, nh * hd), lambda b, g: (b, 0, g)),
                  pl.BlockSpec((1, past, nh * hd), lambda b, g: (b, 0, g)),
                  pl.BlockSpec((Tn, nh * hd), lambda b, g: (b, g)),
                  pl.BlockSpec((Tn, nh * hd), lambda b, g: (b, g)),
                  pl.BlockSpec((1, nh, f_t.shape[2]), lambda b, g: (b, g, 0))],
        out_specs=pl.BlockSpec((Tn, nh * hd), lambda b, g: (b, g)),
        out_shape=jax.ShapeDtypeStruct((Bsz * Tn, heads * hd), BF16),
        compiler_params=_params("parallel", "parallel"),
        name="attn_sample",
    )(q, ck, cv, k_new, v_new, f_t)


def _trunk(x, conv_st, ssm_st, cache_k, cache_v, cache_logf, w, p):
    Bsz, T, D = x.shape
    M = Bsz * T
    heads, hd = cache_k.shape[2], cache_k.shape[3]
    d_inner = p["a_norm"].shape[-1]
    d_ff = p["d_ff"]
    n_state = ssm_st.shape[-1]
    groups = (conv_st.shape[-1] - d_inner) // (2 * n_state)
    d_conv = d_inner + 2 * groups * n_state
    L = SSD_CHUNK if T % SSD_CHUNK == 0 else T
    small = M <= 512
    wb = {}

    norm_ffn = p["norm_ffn"].reshape(-1, D)
    tn_r = 256 if small else 512

    def ffn(h, xn, idx, next_gains):
        a, wb[f"wgu{idx}"] = mm_gateup(xn, w[f"wgu{idx}"], n_cols=d_ff)
        h, xns, wb[f"wd{idx}"] = mm_resid(a, w[f"wd{idx}"], h, 0.5, next_gains, tn=tn_r, name="mm_down")
        return h, xns

    h = x.reshape(M, D)
    h, (u,) = ffn(h, norm_prep(h, norm_ffn[0]), 0, [p["norm_mix"][0]])
    tn_in = 512 if small else 1024
    (gate,), wb["w_z"] = mm_plain(u, w["w_z"], [F32], n_cols=d_inner, tn=tn_in, silu=True, name="mm_z")
    xbc, new_conv, wb["w_xbc"] = mm_conv(u, w["w_xbc"], p["a_conv_w"][0], p["a_conv_b"][0], conv_st[0],
                                         T=T, n_cols=d_conv, tn=tn_in)
    dt, acum, wb["w_dt"] = mm_dt(u, w["w_dt"], p["a_dt_bias"][0], p["a_A_log"][0], L=L)
    y, new_ssm = ssd_mixer(gate, xbc, dt, acum, ssm_st[0], p["a_D"][0], p["a_norm"][0],
                           Bsz=Bsz, T=T, L=L, d_inner=d_inner, groups=groups, n_state=n_state)
    h, (xn,), wb["w_out"] = mm_resid(y, w["w_out"], h, 1.0, [norm_ffn[1]], tm=1024, tn=256, name="mm_out_proj")
    h, (x_kv, xn) = ffn(h, xn, 1, [p["kv_norm"], norm_ffn[2]])
    tn = 512 if small else 1024
    (k32, kb), wb["w_k"] = mm_plain(x_kv, w["w_k"], [F32, BF16], n_cols=D, tn=tn, name="mm_k")
    (v32, vb), wb["w_v"] = mm_plain(x_kv, w["w_v"], [F32, BF16], n_cols=D, tn=tn, name="mm_v")
    logf = mm_logf(x_kv, p["w_f"], p["b_f"])
    f_t = fcum(cache_logf, logf, Bsz=Bsz, Tn=T)
    h, (u,) = ffn(h, xn, 2, [p["norm_mix"][1]])
    (q,), wb["w_q"] = mm_plain(u, w["w_q"], [BF16], n_cols=D, tn=tn, name="mm_q")
    if cache_k.shape[1] == 0:
        o = attn_prompt(q, kb, vb, f_t, Bsz=Bsz, T=T, heads=heads, hd=hd)
    else:
        o = attn_sample(q, cache_k, cache_v, kb, vb, f_t, Bsz=Bsz, Tn=T, heads=heads, hd=hd)
    h, (xn,), wb["w_o"] = mm_resid(o, w["w_o"], h, 1.0, [norm_ffn[3]], tm=1024, tn=512, name="mm_wo")
    h, _ = ffn(h, xn, 3, [])
    y_out = rmsnorm(h, p["norm_final"], F32).reshape(Bsz, T, D)
    outs = (y_out, new_conv[None], new_ssm[None],
            k32.reshape(Bsz, T, heads, hd), v32.reshape(Bsz, T, heads, hd),
            logf[:, :heads].reshape(Bsz, T, heads))
    return outs, wb


def kernel(x_prompt, x_sample, state_conv, state_ssm, cache_k, cache_v, cache_logf, norm_ffn, ffn_wg, ffn_wu,
           ffn_wd, norm_mix, a_w_in, a_conv_w, a_conv_b, a_dt_bias, a_A_log, a_D, a_norm, a_w_out, kv_norm,
           w_kvf, b_f, b_wq, b_wo, norm_final):
    assert norm_mix.shape[0] == 2 and a_w_in.shape[0] == 1 and b_wq.shape[0] == 1
    D = x_prompt.shape[-1]
    F = ffn_wg.shape[-1]
    heads = cache_k.shape[2]
    d_inner = a_norm.shape[-1]
    d_zxbc = a_w_in.shape[-1] - a_dt_bias.shape[-1]
    p = {"norm_ffn": norm_ffn, "norm_mix": norm_mix, "a_conv_w": a_conv_w, "a_conv_b": a_conv_b,
         "a_dt_bias": a_dt_bias, "a_A_log": a_A_log, "a_D": a_D, "a_norm": a_norm, "kv_norm": kv_norm,
         "norm_final": norm_final, "d_ff": F,
         "w_f": jnp.pad(w_kvf[:, 2 * D:], ((0, 0), (0, LANES - heads))).astype(BF16),
         "b_f": jnp.pad(b_f, (0, LANES - heads)).reshape(1, LANES)}
    wg, wu, wd = ffn_wg.reshape(-1, F), ffn_wu.reshape(-1, F), ffn_wd.reshape(-1, D)
    w = {"w_z": (a_w_in[0], 0, 0), "w_xbc": (a_w_in[0], 0, d_inner), "w_dt": (a_w_in[0], 0, d_zxbc),
         "w_out": (a_w_out[0], 0, 0),
         "w_k": (w_kvf, 0, 0), "w_v": (w_kvf, 0, D), "w_q": (b_wq[0], 0, 0), "w_o": (b_wo[0], 0, 0)}
    for idx in range(wg.shape[0] // D):
        w[f"wgu{idx}"], w[f"wd{idx}"] = ((wg, idx, 0), (wu, idx, 0)), (wd, idx, 0)

    bp = x_prompt.shape[0]
    zero_conv = jnp.zeros((state_conv.shape[0], bp) + state_conv.shape[2:], state_conv.dtype)
    zero_ssm = jnp.zeros((state_ssm.shape[0], bp) + state_ssm.shape[2:], state_ssm.dtype)
    empty_kv = jnp.zeros((bp, 0) + cache_k.shape[2:], cache_k.dtype)
    empty_lf = jnp.zeros((bp, 0, heads), cache_logf.dtype)
    (y_s, s_conv, s_ssm, s_k, s_v, s_logf), wb = _trunk(
        x_sample, state_conv, state_ssm, cache_k, cache_v, cache_logf, w, p)
    (y_p, p_conv, p_ssm, p_k, p_v, p_logf), _ = _trunk(
        x_prompt, zero_conv, zero_ssm, empty_kv, empty_kv, empty_lf, wb, p)
    return (y_p, y_s, p_conv, p_ssm, p_k, p_v, p_logf, s_conv, s_ssm, s_k, s_v, s_logf)
```

```python
import functools

import numpy as np
import jax
import jax.numpy as jnp
from jax import lax
from jax.experimental import pallas as pl
from jax.experimental.pallas import tpu as pltpu

NORM_EPS = 1e-5
SSD_CHUNK = 64
SSM_HEAD_DIM = 64
CONV_W = 4
LANES = 128
LOG2E = 1.4426950408889634
VMEM_LIMIT_BYTES = 56 * 1024 * 1024
CONV_VMEM_LIMIT_BYTES = 60 * 1024 * 1024
SSD_GROUPS_PER_STEP = 4
GATEUP_ROW_CHUNK = 512
CONV_COL_CHUNK = 256
ATTN_BLOCK = 256
ATTN_HEADS_PER_STEP = 4
BF16 = jnp.bfloat16
F32 = jnp.float32


def _params(*sem, vmem=VMEM_LIMIT_BYTES):
    return pltpu.CompilerParams(dimension_semantics=sem, vmem_limit_bytes=vmem)


def _tile(n, pref):
    if n <= pref:
        return n
    t = 1 << (pref.bit_length() - 1)
    while n % t:
        t //= 2
    return t


def _sigmoid(x):
    return 1.0 / (1.0 + jnp.exp(-x))


def _silu(x):
    return x * _sigmoid(x)


def _softplus(x):
    return jnp.maximum(x, 0.0) + jnp.log1p(jnp.exp(-jnp.abs(x)))


def _cumsum_rows(x, seg):
    rowi = jnp.bitwise_and(lax.broadcasted_iota(jnp.int32, x.shape, 0), seg - 1)
    s = 1
    while s < seg:
        x = x + jnp.where(rowi >= s, pltpu.roll(x, s, 0), 0.0)
        s *= 2
    return x


def _rmsnorm_kernel(x_ref, g_ref, o_ref):
    x = x_ref[...]
    y = x * lax.rsqrt(jnp.mean(x * x, axis=-1, keepdims=True) + NORM_EPS)
    o_ref[...] = (y * g_ref[...]).astype(o_ref.dtype)


def rmsnorm(x, g, out_dtype):
    M, D = x.shape
    tm = _tile(M, 512)
    return pl.pallas_call(
        _rmsnorm_kernel,
        grid=(M // tm,),
        in_specs=[pl.BlockSpec((tm, D), lambda i: (i, 0)),
                  pl.BlockSpec((1, D), lambda i: (0, 0))],
        out_specs=pl.BlockSpec((tm, D), lambda i: (i, 0)),
        out_shape=jax.ShapeDtypeStruct((M, D), out_dtype),
        compiler_params=_params("parallel"),
        name="rmsnorm",
    )(x, g.reshape(1, D))


def _fold_lanes(v):
    parts = [v[:, k * LANES:(k + 1) * LANES] for k in range(v.shape[1] // LANES)]
    while len(parts) > 1:
        parts = [a + b for a, b in zip(parts[::2], parts[1::2])] + parts[len(parts) & ~1:]
    return parts[0]


def _row_scale(ssq, d):
    r = lax.rsqrt(jnp.sum(ssq, axis=-1, keepdims=True) * (1.0 / d) + NORM_EPS)
    return jnp.broadcast_to(r, ssq.shape)


def _scale_rows(acc, rs_ref):
    return acc * jnp.concatenate([rs_ref[...]] * (acc.shape[1] // LANES), axis=1)


def _norm_prep_kernel(x_ref, g_ref, hb_ref, rs_ref):
    x = x_ref[...]
    hb_ref[...] = (x * g_ref[...]).astype(hb_ref.dtype)
    rs_ref[...] = _row_scale(_fold_lanes(x * x), x.shape[1])


def norm_prep(x, g):
    M, D = x.shape
    tm = _tile(M, 512)
    hb, rs = pl.pallas_call(
        _norm_prep_kernel,
        grid=(M // tm,),
        in_specs=[pl.BlockSpec((tm, D), lambda i: (i, 0)),
                  pl.BlockSpec((1, D), lambda i: (0, 0))],
        out_specs=[pl.BlockSpec((tm, D), lambda i: (i, 0)),
                   pl.BlockSpec((tm, LANES), lambda i: (i, 0))],
        out_shape=[jax.ShapeDtypeStruct((M, D), BF16), jax.ShapeDtypeStruct((M, LANES), F32)],
        compiler_params=_params("parallel"),
        name="norm_prep",
    )(x, g.reshape(1, D))
    return hb, rs


def _w_bf16(w_ref, wb_ref):
    w = w_ref[...]
    if wb_ref is not None:
        w = w.astype(BF16)
        wb_ref[...] = w
    return w


def _w_specs(w, K, n_cols, tn, M, tm):
    arr, row_blk, col_off = w
    emit = arr.dtype != BF16
    assert col_off % tn == 0 and n_cols % tn == 0
    assert not emit or M == tm
    cb = col_off // tn
    in_spec = pl.BlockSpec((K, tn), lambda i, j: (row_blk, cb + j))
    out_specs = [pl.BlockSpec((K, tn), lambda i, j: (0, j))] if emit else []
    out_shapes = [jax.ShapeDtypeStruct((K, n_cols), BF16)] if emit else []
    return arr, emit, in_spec, out_specs, out_shapes


def _x_specs(xn, tm):
    hb, _ = xn
    return [pl.BlockSpec((tm, hb.shape[1]), lambda i, j: (i, 0)),
            pl.BlockSpec((tm, LANES), lambda i, j: (i, 0))]


def _mm_plain_kernel(x_ref, rs_ref, w_ref, *o_refs, emit, silu):
    wb_ref = o_refs[-1] if emit else None
    acc = jnp.dot(x_ref[...], _w_bf16(w_ref, wb_ref), preferred_element_type=F32)
    acc = _scale_rows(acc, rs_ref)
    if silu:
        acc = _silu(acc)
    for o_ref in o_refs[:len(o_refs) - emit]:
        o_ref[...] = acc.astype(o_ref.dtype)


def mm_plain(xn, w, out_dtypes, *, n_cols, tm=1024, tn=1024, silu=False, name="mm_plain"):
    M, K = xn[0].shape
    tm, tn = _tile(M, tm), _tile(n_cols, tn)
    arr, emit, w_spec, wb_specs, wb_shapes = _w_specs(w, K, n_cols, tn, M, tm)
    outs = pl.pallas_call(
        functools.partial(_mm_plain_kernel, emit=emit, silu=silu),
        grid=(M // tm, n_cols // tn),
        in_specs=_x_specs(xn, tm) + [w_spec],
        out_specs=[pl.BlockSpec((tm, tn), lambda i, j: (i, j)) for _ in out_dtypes] + wb_specs,
        out_shape=[jax.ShapeDtypeStruct((M, n_cols), dt) for dt in out_dtypes] + wb_shapes,
        compiler_params=_params("parallel", "arbitrary"),
        name=name,
    )(*xn, arr)
    n = len(out_dtypes)
    return outs[:n], ((outs[n], 0, 0) if emit else w)


def _mm_conv_kernel(x_ref, rs_ref, w_ref, cw_ref, cb_ref, st_ref, o_ref, tail_ref, *rest,
                    emit, seg_len, tiles_per_seq):
    pad_scr, carry_scr = rest[-2:]
    i, j = pl.program_id(0), pl.program_id(1)
    tm, tn = o_ref.shape
    if emit:
        rest[0][...] = w_ref[...].astype(BF16)
        w_ref = rest[0]
    tail_ref[...] = jnp.zeros_like(tail_ref)

    def conv_silu(rows, cols):
        acc = cb_ref[:, cols] + pad_scr[5:5 + rows, cols] * cw_ref[0:1, cols]
        for k in range(1, CONV_W):
            acc = acc + pad_scr[5 + k:5 + k + rows, cols] * cw_ref[k:k + 1, cols]
        return _silu(acc)

    if seg_len == tm:
        prev = st_ref[0]
        if tiles_per_seq > 1:
            @pl.when(i == 0)
            def _():
                carry_scr[j] = jnp.zeros(carry_scr.shape[1:], F32)

            prev = jnp.where(i % tiles_per_seq == 0, prev, carry_scr[j, 0:CONV_W - 1, :])
        pad_scr[5:8, :] = prev
        x = x_ref[...]
        rs = jnp.concatenate([rs_ref[...]] * (CONV_COL_CHUNK // LANES), axis=1)
        for c0 in range(0, tn, CONV_COL_CHUNK):
            cols = slice(c0, c0 + CONV_COL_CHUNK)
            pad_scr[8:8 + tm, cols] = jnp.dot(x, w_ref[:, cols], preferred_element_type=F32) * rs
            o_ref[:, cols] = conv_silu(tm, cols)
        last = pad_scr[5 + tm:8 + tm, :]
        tail_ref[0, 0:CONV_W - 1, :] = last
        if tiles_per_seq > 1:
            carry_scr[j, 0:CONV_W - 1, :] = last
    else:
        raw = _scale_rows(jnp.dot(x_ref[...], w_ref[...], preferred_element_type=F32), rs_ref)
        for s in range(tm // seg_len):
            pad_scr[5:8, :] = st_ref[s]
            pad_scr[8:8 + seg_len, :] = raw[s * seg_len:(s + 1) * seg_len]
            o_ref[s * seg_len:(s + 1) * seg_len, :] = conv_silu(seg_len, slice(0, tn))
            tail_ref[s, 0:CONV_W - 1, :] = pad_scr[5 + seg_len:8 + seg_len, :]


def mm_conv(xn, w, conv_w, conv_b, conv_state, *, T, n_cols, tm=1024, tn=1024):
    M, K = xn[0].shape
    Bsz = M // T
    tm, tn = _tile(M, tm), _tile(n_cols, tn)
    seg_len = min(T, tm)
    assert tm % seg_len == 0 and T % seg_len == 0 and seg_len >= CONV_W - 1
    nseg, tiles_per_seq = tm // seg_len, T // seg_len
    arr, emit, w_spec, wb_specs, wb_shapes = _w_specs(w, K, n_cols, tn, M, tm)
    outs = pl.pallas_call(
        functools.partial(_mm_conv_kernel, emit=emit, seg_len=seg_len, tiles_per_seq=tiles_per_seq),
        grid=(M // tm, n_cols // tn),
        in_specs=_x_specs(xn, tm) + [
            w_spec,
            pl.BlockSpec((CONV_W, tn), lambda i, j: (0, j)),
            pl.BlockSpec((1, tn), lambda i, j: (0, j)),
            pl.BlockSpec((nseg, CONV_W - 1, tn), lambda i, j: (i // tiles_per_seq, 0, j))],
        out_specs=[pl.BlockSpec((tm, tn), lambda i, j: (i, j)),
                   pl.BlockSpec((nseg, 8, tn), lambda i, j: (i, 0, j))] + wb_specs,
        out_shape=[jax.ShapeDtypeStruct((M, n_cols), F32),
                   jax.ShapeDtypeStruct((M // seg_len, 8, n_cols), F32)] + wb_shapes,
        scratch_shapes=[pltpu.VMEM((8 + tm if nseg == 1 else 8 + seg_len, tn), F32),
                        pltpu.VMEM((n_cols // tn, 8, tn), F32)],
        compiler_params=_params("arbitrary", "arbitrary", vmem=CONV_VMEM_LIMIT_BYTES),
        name="mm_conv",
    )(*xn, arr, conv_w, conv_b.reshape(1, n_cols), conv_state)
    tails = outs[1].reshape(Bsz, tiles_per_seq, 8, n_cols)[:, -1, 0:CONV_W - 1, :]
    return outs[0], tails, ((outs[2], 0, 0) if emit else w)


def _mm_resid_kernel(*refs, alpha, n_norm, emit, n_total):
    x_ref, w_ref, r_ref = refs[:3]
    g_refs = refs[3:3 + n_norm]
    o_ref = refs[3 + n_norm]
    hb_refs = refs[4 + n_norm:4 + 2 * n_norm]
    rs_ref = refs[4 + 2 * n_norm] if n_norm else None
    wb_ref = refs[-1] if emit else None
    acc = jnp.dot(x_ref[...], _w_bf16(w_ref, wb_ref), preferred_element_type=F32)
    h = r_ref[...] + alpha * acc
    o_ref[...] = h
    for g_ref, hb_ref in zip(g_refs, hb_refs):
        hb_ref[...] = (h * g_ref[...]).astype(hb_ref.dtype)
    if n_norm:
        part = _fold_lanes(h * h)
        j = pl.program_id(1)

        @pl.when(j == 0)
        def _():
            rs_ref[...] = part

        @pl.when(j > 0)
        def _():
            rs_ref[...] += part

        @pl.when(j == pl.num_programs(1) - 1)
        def _():
            rs_ref[...] = _row_scale(rs_ref[...], n_total)


def mm_resid(x, w, resid, alpha, gains=(), *, tm=512, tn=512, name="mm_resid"):
    M, K = x.shape
    N = resid.shape[1]
    tm, tn = _tile(M, tm), _tile(N, tn)
    n_norm = len(gains)
    arr, emit, w_spec, wb_specs, wb_shapes = _w_specs(w, K, N, tn, M, tm)
    tile = pl.BlockSpec((tm, tn), lambda i, j: (i, j))
    outs = pl.pallas_call(
        functools.partial(_mm_resid_kernel, alpha=alpha, n_norm=n_norm, emit=emit, n_total=N),
        grid=(M // tm, N // tn),
        in_specs=[pl.BlockSpec((tm, K), lambda i, j: (i, 0)), w_spec, tile]
                 + [pl.BlockSpec((1, tn), lambda i, j: (0, j))] * n_norm,
        out_specs=[tile] * (1 + n_norm) + [pl.BlockSpec((tm, LANES), lambda i, j: (i, 0))] * bool(n_norm) + wb_specs,
        out_shape=[jax.ShapeDtypeStruct((M, N), F32)] + [jax.ShapeDtypeStruct((M, N), BF16)] * n_norm
                  + [jax.ShapeDtypeStruct((M, LANES), F32)] * bool(n_norm) + wb_shapes,
        compiler_params=_params("parallel", "arbitrary"),
        name=name,
    )(x, arr, resid, *[g.reshape(1, N) for g in gains])
    xns = [(outs[1 + k], outs[1 + n_norm]) for k in range(n_norm)]
    return outs[0], xns, ((outs[-1], 0, 0) if emit else w)


def _gateup_epilogue(g, u, rs_ref, o_ref):
    o_ref[...] = (_silu(_scale_rows(g, rs_ref)) * _scale_rows(u, rs_ref)).astype(o_ref.dtype)


def _mm_gateup_pair_kernel(x_ref, rs_ref, wg_ref, wu_ref, o_ref, *wgu_refs):
    x = x_ref[...]
    wg, wu = wg_ref[...], wu_ref[...]
    if wgu_refs:
        tn = wg.shape[1]
        wg, wu = wg.astype(BF16), wu.astype(BF16)
        wgu_refs[0][:, 0:tn] = wg
        wgu_refs[0][:, tn:2 * tn] = wu
    g = jnp.dot(x, wg, preferred_element_type=F32)
    u = jnp.dot(x, wu, preferred_element_type=F32)
    _gateup_epilogue(g, u, rs_ref, o_ref)


def _mm_gateup_kernel(x_ref, rs_ref, wgu_ref, o_ref):
    tm, tn = o_ref.shape
    rows = min(tm, GATEUP_ROW_CHUNK)
    w = wgu_ref[...]
    for r0 in range(0, tm, rows):
        rsl = slice(r0, r0 + rows)
        gu = jnp.dot(x_ref[rsl, :], w, preferred_element_type=F32)
        rs = jnp.concatenate([rs_ref[rsl, :]] * (tn // LANES), axis=1)
        o_ref[rsl, :] = (_silu(gu[:, 0:tn] * rs) * (gu[:, tn:2 * tn] * rs)).astype(o_ref.dtype)


def mm_gateup(xn, wgu, *, n_cols, tm=2048, tn=256):
    M, K = xn[0].shape
    tm, tn = _tile(M, tm), _tile(n_cols, tn)
    out_spec = pl.BlockSpec((tm, tn), lambda i, j: (i, j))
    out_shape = jax.ShapeDtypeStruct((M, n_cols), BF16)
    if isinstance(wgu[0], tuple):
        garr, emit, g_spec, _, _ = _w_specs(wgu[0], K, n_cols, tn, M, tm)
        uarr, _, u_spec, _, _ = _w_specs(wgu[1], K, n_cols, tn, M, tm)
        outs = pl.pallas_call(
            _mm_gateup_pair_kernel,
            grid=(M // tm, n_cols // tn),
            in_specs=_x_specs(xn, tm) + [g_spec, u_spec],
            out_specs=[out_spec] + [pl.BlockSpec((K, 2 * tn), lambda i, j: (0, j))] * emit,
            out_shape=[out_shape] + [jax.ShapeDtypeStruct((K, 2 * n_cols), BF16)] * emit,
            compiler_params=_params("parallel", "arbitrary"),
            name="mm_gateup",
        )(*xn, garr, uarr)
        return outs[0], ((outs[1], 0, 0) if emit else wgu)
    arr, row_blk, col_off = wgu
    assert row_blk == 0 and col_off == 0 and arr.shape == (K, 2 * n_cols)
    out = pl.pallas_call(
        _mm_gateup_kernel,
        grid=(M // tm, n_cols // tn),
        in_specs=_x_specs(xn, tm) + [pl.BlockSpec((K, 2 * tn), lambda i, j: (0, j))],
        out_specs=out_spec,
        out_shape=out_shape,
        compiler_params=_params("parallel", "arbitrary"),
        name="mm_gateup",
    )(*xn, arr)
    return out, wgu


def _mm_dt_kernel(x_ref, rs_ref, w_ref, b_ref, alog_ref, dt_ref, acum_ref, *wb_refs, L):
    raw = jnp.dot(x_ref[...], _w_bf16(w_ref, wb_refs[0] if wb_refs else None), preferred_element_type=F32)
    dt = _softplus(_scale_rows(raw, rs_ref) + b_ref[...])
    dt_ref[...] = dt
    acum_ref[...] = _cumsum_rows(dt * (-jnp.exp(alog_ref[...])), L)


def mm_dt(xn, w, dt_bias, a_log, *, L, tm=2048):
    M, K = xn[0].shape
    heads = dt_bias.shape[0]
    tm = _tile(M, tm)
    assert tm % L == 0 and L & (L - 1) == 0
    arr, emit, w_spec, wb_specs, wb_shapes = _w_specs(w, K, heads, heads, M, tm)
    vec = pl.BlockSpec((1, heads), lambda i, j: (0, 0))
    outs = pl.pallas_call(
        functools.partial(_mm_dt_kernel, L=L),
        grid=(M // tm, 1),
        in_specs=_x_specs(xn, tm) + [w_spec, vec, vec],
        out_specs=[pl.BlockSpec((tm, heads), lambda i, j: (i, 0))] * 2 + wb_specs,
        out_shape=[jax.ShapeDtypeStruct((M, heads), F32)] * 2 + wb_shapes,
        compiler_params=_params("parallel", "arbitrary"),
        name="mm_dt",
    )(*xn, arr, dt_bias.reshape(1, heads), a_log.reshape(1, heads))
    return outs[0], outs[1], ((outs[2], 0, 0) if emit else w)


def _mm_logf_kernel(x_ref, rs_ref, w_ref, b_ref, o_ref):
    s = jnp.dot(x_ref[...], w_ref[...], preferred_element_type=F32)
    s = _scale_rows(s, rs_ref) + b_ref[...]
    o_ref[...] = -_softplus(-s)


def mm_logf(xn, w, b):
    M, K = xn[0].shape
    N = w.shape[1]
    tm = _tile(M, 2048)
    return pl.pallas_call(
        _mm_logf_kernel,
        grid=(M // tm, 1),
        in_specs=_x_specs(xn, tm) + [pl.BlockSpec((K, N), lambda i, j: (0, 0)),
                                     pl.BlockSpec((1, N), lambda i, j: (0, 0))],
        out_specs=pl.BlockSpec((tm, N), lambda i, j: (i, 0)),
        out_shape=jax.ShapeDtypeStruct((M, N), F32),
        compiler_params=_params("parallel", "arbitrary"),
        name="mm_logf",
    )(*xn, w, b)


def _split3(v):
    hi = v.astype(BF16)
    r1 = v - hi.astype(F32)
    mid = r1.astype(BF16)
    lo = (r1 - mid.astype(F32)).astype(BF16)
    return [hi, mid, lo]


def _ssd_kernel(gate_ref, x_ref, b_ref, c_ref, dt_ref, acum_ref, dx_ref, ng_ref, e_ref, h0_ref,
                y_ref, hout_ref, ht_scr, *, L, GW):
    c = pl.program_id(2)
    P = SSM_HEAD_DIM
    N = LANES
    n_sub = x_ref.shape[1] // GW

    @pl.when(c == 0)
    def _():
        for s in range(n_sub):
            ht_scr[s] = h0_ref[0, s * GW:(s + 1) * GW, :].T

    lhs = jnp.concatenate(_split3(acum_ref[...]) + _split3(dt_ref[...]), axis=0)

    lane = lax.broadcasted_iota(jnp.int32, (L, LANES), 1)
    rowi = lax.broadcasted_iota(jnp.int32, (L, LANES), 0)
    s_idx = jnp.bitwise_and(lane, P - 1)
    r_idx = jnp.right_shift(lane, P.bit_length() - 1)
    eye2 = rowi == s_idx
    tri2 = s_idx <= rowi
    pad = [] if L == P else [jnp.zeros((P - L, LANES), BF16)]

    def early_dots(s):
        Bb = b_ref[:, s * N:(s + 1) * N].astype(BF16)
        Cb = c_ref[:, s * N:(s + 1) * N].astype(BF16)
        ht = ht_scr[s]
        ex = jnp.dot(lhs, e_ref[s], preferred_element_type=F32)
        ch = jnp.dot(Cb, ht.astype(BF16), preferred_element_type=F32)
        b2 = jnp.concatenate([Bb] + pad + [Bb] + pad, axis=0)
        cb2 = lax.dot_general(Cb, b2, (((1,), (1,)), ((), ())), preferred_element_type=F32)
        return Bb, ht, ex, ch, cb2

    early = {0: early_dots(0)}
    for s in range(n_sub):
        if s + 1 < n_sub:
            early[s + 1] = early_dots(s + 1)
        Bb, ht, ex, ch, cb2 = early.pop(s)
        xs = x_ref[:, s * GW:(s + 1) * GW]
        acum_x = ex[0:L] + ex[L:2 * L] + ex[2 * L:3 * L]
        dt_x = ex[3 * L:4 * L] + ex[4 * L:5 * L] + ex[5 * L:6 * L]

        xdt = xs * dt_x
        y_inter = ch * jnp.exp(acum_x)

        y_parts = []
        for j in range(GW // LANES):
            sl = slice(j * LANES, (j + 1) * LANES)
            ax = acum_x[:, sl]
            v = jnp.sum(jnp.where(eye2, ax, 0.0), axis=0, keepdims=True)
            decay = jnp.exp(jnp.where(tri2, ax - v, -jnp.inf))
            m2 = (cb2 * decay).astype(BF16)
            xp = xdt[:, sl]
            top = jnp.where(r_idx == 0, xp, 0.0).astype(BF16)
            bot = jnp.where(r_idx == 1, xp, 0.0).astype(BF16)
            bd = jnp.concatenate([top] + pad + [bot] + pad, axis=0)
            y_parts.append(jnp.dot(m2, bd, preferred_element_type=F32))
        y = jnp.concatenate(y_parts, axis=1) + y_inter + dx_ref[:, s * GW:(s + 1) * GW] * xs

        y = y * gate_ref[:, s * GW:(s + 1) * GW]
        y = y * lax.rsqrt(jnp.mean(y * y, axis=-1, keepdims=True) + NORM_EPS)
        y_ref[:, s * GW:(s + 1) * GW] = (y * ng_ref[:, s * GW:(s + 1) * GW]).astype(y_ref.dtype)

        last = acum_x[L - 1:L, :]
        xt = (xdt * jnp.exp(last - acum_x)).astype(BF16)
        upd = lax.dot_general(Bb, xt, (((0,), (0,)), ((), ())), preferred_element_type=F32)
        ht_scr[s] = ht * jnp.exp(last) + upd

    @pl.when(c == pl.num_programs(2) - 1)
    def _():
        for s in range(n_sub):
            hout_ref[0, s * GW:(s + 1) * GW, :] = ht_scr[s].T


def ssd_mixer(gate, xbc, dt, acum, ssm_state, d_skip, norm_g, *, Bsz, T, L, d_inner, groups, n_state):
    M = Bsz * T
    heads = dt.shape[1]
    P = SSM_HEAD_DIM
    GW = d_inner // groups
    N = n_state
    R = GW // P
    nc = T // L
    ns = SSD_GROUPS_PER_STEP
    assert N == LANES and heads == LANES and GW % LANES == 0 and L % 16 == 0 and L <= P and groups % ns == 0
    SW, SN = ns * GW, ns * N
    kb, kc = d_inner // SN, (d_inner + groups * N) // SN

    expand = np.zeros((groups, heads, GW), np.float32)
    for g in range(groups):
        for r in range(R):
            expand[g, g * R + r, r * P:(r + 1) * P] = 1.0
    expand = jnp.asarray(expand, BF16)

    d_x = jnp.repeat(d_skip, P).reshape(1, d_inner)
    h0 = ssm_state.reshape(Bsz, heads * P, N)

    row = lambda b, g, c: b * nc + c
    in_specs = [
        pl.BlockSpec((L, SW), lambda b, g, c: (row(b, g, c), g)),
        pl.BlockSpec((L, SW), lambda b, g, c: (row(b, g, c), g)),
        pl.BlockSpec((L, SN), lambda b, g, c: (row(b, g, c), kb + g)),
        pl.BlockSpec((L, SN), lambda b, g, c: (row(b, g, c), kc + g)),
        pl.BlockSpec((L, heads), lambda b, g, c: (row(b, g, c), 0)),
        pl.BlockSpec((L, heads), lambda b, g, c: (row(b, g, c), 0)),
        pl.BlockSpec((1, SW), lambda b, g, c: (0, g)),
        pl.BlockSpec((1, SW), lambda b, g, c: (0, g)),
        pl.BlockSpec((ns, heads, GW), lambda b, g, c: (g, 0, 0)),
        pl.BlockSpec((1, SW, N), lambda b, g, c: (b, g, 0)),
    ]
    out_specs = [
        pl.BlockSpec((L, SW), lambda b, g, c: (row(b, g, c), g)),
        pl.BlockSpec((1, SW, N), lambda b, g, c: (b, g, 0)),
    ]
    out_shape = [
        jax.ShapeDtypeStruct((M, d_inner), BF16),
        jax.ShapeDtypeStruct((Bsz, heads * P, N), F32),
    ]
    y, h_new = pl.pallas_call(
        functools.partial(_ssd_kernel, L=L, GW=GW),
        grid=(Bsz, groups // ns, nc),
        in_specs=in_specs,
        out_specs=out_specs,
        out_shape=out_shape,
        scratch_shapes=[pltpu.VMEM((ns, N, GW), F32)],
        compiler_params=_params("parallel", "parallel", "arbitrary"),
        name="ssd_mixer",
    )(gate, xbc, xbc, xbc, dt, acum, d_x, norm_g.reshape(1, d_inner), expand, h0)
    return y, h_new.reshape(Bsz, heads, P, N)


def _fcum_kernel(past_ref, new_ref, o_ref, w_scr, *, past, Tn, heads):
    w_scr[...] = jnp.zeros_like(w_scr)
    if past:
        w_scr[0:past, 0:heads] = past_ref[0]
    w_scr[past:past + Tn, :] = new_ref[...]
    n = w_scr.shape[0]
    o_ref[0] = _cumsum_rows(w_scr[...], 1 << (n - 1).bit_length()).T


def fcum(cache_logf, logf_new, *, Bsz, Tn):
    past, heads = cache_logf.shape[1], cache_logf.shape[2]
    t_pad = -(-(past + Tn) // LANES) * LANES
    if past == 0:
        cache_logf = jnp.zeros((Bsz, 8, heads), F32)
    return pl.pallas_call(
        functools.partial(_fcum_kernel, past=past, Tn=Tn, heads=heads),
        grid=(Bsz,),
        in_specs=[pl.BlockSpec((1,) + cache_logf.shape[1:], lambda b: (b, 0, 0)),
                  pl.BlockSpec((Tn, LANES), lambda b: (b, 0))],
        out_specs=pl.BlockSpec((1, LANES, t_pad), lambda b: (b, 0, 0)),
        out_shape=jax.ShapeDtypeStruct((Bsz, LANES, t_pad), F32),
        scratch_shapes=[pltpu.VMEM((t_pad, LANES), F32)],
        compiler_params=_params("parallel"),
        name="fcum",
    )(cache_logf, logf_new)


def _row_to_cols(frow):
    return jnp.broadcast_to(frow, (LANES, frow.shape[1])).T


def _attn_prompt_kernel(q_ref, k_ref, v_ref, f_ref, o_ref, vt_scr, fk_scr, *, tb, hd, scale):
    T = q_ref.shape[0]
    nh = q_ref.shape[1] // hd
    nb = T // tb
    h0 = (pl.program_id(1) % (8 // nh)) * nh
    scale = scale * LOG2E
    frows = [f_ref[0, pl.ds(h0 + i, 1), :] * LOG2E for i in range(nh)]
    wide = 2 * tb
    for i in range(nh):
        fk_scr[i] = _row_to_cols(frows[i])
        for j in range(T // wide):
            vt_scr[i, j] = v_ref[j * wide:(j + 1) * wide, i * hd:(i + 1) * hd].astype(F32).T.astype(BF16)
    kofs = lax.broadcasted_iota(jnp.int32, (tb, tb), 0)
    qofs = lax.broadcasted_iota(jnp.int32, (tb, tb), 1)

    for qi in range(nb):
        qs = slice(qi * tb, (qi + 1) * tb)
        q_t = [q_ref[qs, i * hd:(i + 1) * hd].astype(F32).T.astype(BF16) for i in range(nh)]

        def block(start, width, vt, carry, masked):
            qk = [jnp.dot(k_ref[pl.ds(start, width), i * hd:(i + 1) * hd], q_t[i], preferred_element_type=F32)
                  for i in range(nh)]
            out = []
            for i in range(nh):
                m, l, acc = carry[3 * i:3 * i + 3]
                fk = fk_scr[i, pl.ds(start, width), :]
                s = qk[i] * scale
                s = s + frows[i][:, qs] - jnp.concatenate([fk] * (tb // LANES), axis=1)
                if masked:
                    s = jnp.where(kofs <= qofs, s, -jnp.inf)
                m_new = jnp.maximum(m, jnp.max(s, axis=0, keepdims=True))
                alpha = jnp.exp2(m - m_new)
                p = jnp.exp2(s - m_new)
                l = alpha * l + jnp.sum(p, axis=0, keepdims=True)
                acc = alpha * acc + jnp.dot(vt(i), p.astype(BF16), preferred_element_type=F32)
                out += [m_new, l, acc]
            return tuple(out)

        def wide_block(jw, carry):
            return block(jw * wide, wide, lambda i: vt_scr[i, jw], carry, False)

        def narrow_block(j, carry, masked):
            half = slice((j % 2) * tb, (j % 2 + 1) * tb)
            return block(j * tb, tb, lambda i: vt_scr[i, j // 2, :, half], carry, masked)

        carry = (jnp.full((1, tb), -jnp.inf, F32), jnp.zeros((1, tb), F32), jnp.zeros((hd, tb), F32)) * nh
        for jw in range(qi // 2):
            carry = wide_block(jw, carry)
        if qi % 2:
            carry = narrow_block(qi - 1, carry, False)
        carry = narrow_block(qi, carry, True)
        for i in range(nh):
            o_ref[qs, i * hd:(i + 1) * hd] = (carry[3 * i + 2] / carry[3 * i + 1]).T.astype(o_ref.dtype)


def attn_prompt(q, k, v, f_t, *, Bsz, T, heads, hd):
    tb = _tile(T, ATTN_BLOCK)
    nh = ATTN_HEADS_PER_STEP
    assert tb % LANES == 0 and T % (2 * tb) == 0 and heads % 8 == 0 and 8 % nh == 0 and hd == LANES
    blk = pl.BlockSpec((T, nh * hd), lambda b, h: (b, h))
    return pl.pallas_call(
        functools.partial(_attn_prompt_kernel, tb=tb, hd=hd, scale=hd ** -0.5),
        grid=(Bsz, heads // nh),
        in_specs=[blk, blk, blk, pl.BlockSpec((1, 8, T), lambda b, h: (b, h * nh // 8, 0))],
        out_specs=blk,
        out_shape=jax.ShapeDtypeStruct((Bsz * T, heads * hd), BF16),
        scratch_shapes=[pltpu.VMEM((nh, T // (2 * tb), hd, 2 * tb), BF16), pltpu.VMEM((nh, T, LANES), F32)],
        compiler_params=_params("parallel", "parallel"),
        name="attn_prompt",
    )(q, k, v, f_t)


def _attn_sample_kernel(q_ref, kc_ref, vc_ref, kn_ref, vn_ref, f_ref, o_ref, *, Tn, past, hd, nh, scale):
    zrows = jnp.zeros((LANES - Tn, hd), BF16)
    rowi = lax.broadcasted_iota(jnp.int32, (Tn, LANES), 0)
    coli = lax.broadcasted_iota(jnp.int32, (Tn, LANES), 1)
    nt = (((1,), (1,)), ((), ()))
    for hh in range(nh):
        sl = slice(hh * hd, (hh + 1) * hd)
        q = q_ref[:, sl]
        kc = kc_ref[0, :, sl].astype(BF16)
        vc = vc_ref[0, :, sl].astype(BF16)
        kn = jnp.concatenate([kn_ref[:, sl], zrows], axis=0)
        vn = jnp.concatenate([vn_ref[:, sl], zrows], axis=0)
        f_past = f_ref[0, hh:hh + 1, 0:past]
        f_new = f_ref[0, hh:hh + 1, past:past + LANES]
        fq_col = _row_to_cols(f_new)[0:Tn, 0:1]
        s1 = lax.dot_general(q, kc, nt, preferred_element_type=F32) * scale + fq_col - f_past
        s2 = lax.dot_general(q, kn, nt, preferred_element_type=F32) * scale + fq_col - f_new
        s2 = jnp.where(coli <= rowi, s2, -jnp.inf)
        m = jnp.maximum(jnp.max(s1, axis=1, keepdims=True), jnp.max(s2, axis=1, keepdims=True))
        p1 = jnp.exp(s1 - m)
        p2 = jnp.exp(s2 - m)
        l = jnp.sum(p1, axis=1, keepdims=True) + jnp.sum(p2, axis=1, keepdims=True)
        o = jnp.dot(p1.astype(BF16), vc, preferred_element_type=F32)
        o = o + jnp.dot(p2.astype(BF16), vn, preferred_element_type=F32)
        o_ref[:, sl] = (o / l).astype(o_ref.dtype)


def attn_sample(q, cache_k, cache_v, k_new, v_new, f_t, *, Bsz, Tn, heads, hd):
    past = cache_k.shape[1]
    nh = 8
    assert past % LANES == 0 and Tn % 16 == 0 and Tn <= LANES and heads % nh == 0
    ck = cache_k.reshape(Bsz, past, heads * hd)
    cv = cache_v.reshape(Bsz, past, heads * hd)
    return pl.pallas_call(
        functools.partial(_attn_sample_kernel, Tn=Tn, past=past, hd=hd, nh=nh, scale=hd ** -0.5),
        grid=(Bsz, heads // nh),
        in_specs=[pl.BlockSpec((Tn, nh * hd), lambda b, g: (b, g)),
                  pl.BlockSpec((1, past, nh * hd), lambda b, g: (b, 0, g)),
                  pl.BlockSpec((1, past, nh * hd), lambda b, g: (b, 0, g)),
                  pl.BlockSpec((Tn, nh * hd), lambda b, g: (b, g)),
                  pl.BlockSpec((Tn, nh * hd), lambda b, g: (b, g)),
                  pl.BlockSpec((1, nh, f_t.shape[2]), lambda b, g: (b, g, 0))],
        out_specs=pl.BlockSpec((Tn, nh * hd), lambda b, g: (b, g)),
        out_shape=jax.ShapeDtypeStruct((Bsz * Tn, heads * hd), BF16),
        compiler_params=_params("parallel", "parallel"),
        name="attn_sample",
    )(q, ck, cv, k_new, v_new, f_t)


def _trunk(x, conv_st, ssm_st, cache_k, cache_v, cache_logf, w, p):
    Bsz, T, D = x.shape
    M = Bsz * T
    heads, hd = cache_k.shape[2], cache_k.shape[3]
    d_inner = p["a_norm"].shape[-1]
    d_ff = p["d_ff"]
    n_state = ssm_st.shape[-1]
    groups = (conv_st.shape[-1] - d_inner) // (2 * n_state)
    d_conv = d_inner + 2 * groups * n_state
    L = SSD_CHUNK if T % SSD_CHUNK == 0 else T
    small = M <= 512
    wb = {}

    norm_ffn = p["norm_ffn"].reshape(-1, D)
    tn_r = 256 if small else 512

    def ffn(h, xn, idx, next_gains):
        a, wb[f"wgu{idx}"] = mm_gateup(xn, w[f"wgu{idx}"], n_cols=d_ff)
        h, xns, wb[f"wd{idx}"] = mm_resid(a, w[f"wd{idx}"], h, 0.5, next_gains, tn=tn_r, name="mm_down")
        return h, xns

    h = x.reshape(M, D)
    h, (u,) = ffn(h, norm_prep(h, norm_ffn[0]), 0, [p["norm_mix"][0]])
    tn_in = 512 if small else 1024
    (gate,), wb["w_z"] = mm_plain(u, w["w_z"], [F32], n_cols=d_inner, tn=tn_in, silu=True, name="mm_z")
    xbc, new_conv, wb["w_xbc"] = mm_conv(u, w["w_xbc"], p["a_conv_w"][0], p["a_conv_b"][0], conv_st[0],
                                         T=T, n_cols=d_conv, tn=tn_in)
    dt, acum, wb["w_dt"] = mm_dt(u, w["w_dt"], p["a_dt_bias"][0], p["a_A_log"][0], L=L)
    y, new_ssm = ssd_mixer(gate, xbc, dt, acum, ssm_st[0], p["a_D"][0], p["a_norm"][0],
                           Bsz=Bsz, T=T, L=L, d_inner=d_inner, groups=groups, n_state=n_state)
    h, (xn,), wb["w_out"] = mm_resid(y, w["w_out"], h, 1.0, [norm_ffn[1]], tm=1024, tn=256, name="mm_out_proj")
    h, (x_kv, xn) = ffn(h, xn, 1, [p["kv_norm"], norm_ffn[2]])
    tn = 512 if small else 1024
    (k32, kb), wb["w_k"] = mm_plain(x_kv, w["w_k"], [F32, BF16], n_cols=D, tn=tn, name="mm_k")
    (v32, vb), wb["w_v"] = mm_plain(x_kv, w["w_v"], [F32, BF16], n_cols=D, tn=tn, name="mm_v")
    logf = mm_logf(x_kv, p["w_f"], p["b_f"])
    f_t = fcum(cache_logf, logf, Bsz=Bsz, Tn=T)
    h, (u,) = ffn(h, xn, 2, [p["norm_mix"][1]])
    (q,), wb["w_q"] = mm_plain(u, w["w_q"], [BF16], n_cols=D, tn=tn, name="mm_q")
    if cache_k.shape[1] == 0:
        o = attn_prompt(q, kb, vb, f_t, Bsz=Bsz, T=T, heads=heads, hd=hd)
    else:
        o = attn_sample(q, cache_k, cache_v, kb, vb, f_t, Bsz=Bsz, Tn=T, heads=heads, hd=hd)
    h, (xn,), wb["w_o"] = mm_resid(o, w["w_o"], h, 1.0, [norm_ffn[3]], tm=1024, tn=512, name="mm_wo")
    h, _ = ffn(h, xn, 3, [])
    y_out = rmsnorm(h, p["norm_final"], F32).reshape(Bsz, T, D)
    outs = (y_out, new_conv[None], new_ssm[None],
            k32.reshape(Bsz, T, heads, hd), v32.reshape(Bsz, T, heads, hd),
            logf[:, :heads].reshape(Bsz, T, heads))
    return outs, wb


def kernel(x_prompt, x_sample, state_conv, state_ssm, cache_k, cache_v, cache_logf, norm_ffn, ffn_wg, ffn_wu,
           ffn_wd, norm_mix, a_w_in, a_conv_w, a_conv_b, a_dt_bias, a_A_log, a_D, a_norm, a_w_out, kv_norm,
           w_kvf, b_f, b_wq, b_wo, norm_final):
    assert norm_mix.shape[0] == 2 and a_w_in.shape[0] == 1 and b_wq.shape[0] == 1
    D = x_prompt.shape[-1]
    F = ffn_wg.shape[-1]
    heads = cache_k.shape[2]
    d_inner = a_norm.shape[-1]
    d_zxbc = a_w_in.shape[-1] - a_dt_bias.shape[-1]
    p = {"norm_ffn": norm_ffn, "norm_mix": norm_mix, "a_conv_w": a_conv_w, "a_conv_b": a_conv_b,
         "a_dt_bias": a_dt_bias, "a_A_log": a_A_log, "a_D": a_D, "a_norm": a_norm, "kv_norm": kv_norm,
         "norm_final": norm_final, "d_ff": F,
         "w_f": jnp.pad(w_kvf[:, 2 * D:], ((0, 0), (0, LANES - heads))).astype(BF16),
         "b_f": jnp.pad(b_f, (0, LANES - heads)).reshape(1, LANES)}
    wg, wu, wd = ffn_wg.reshape(-1, F), ffn_wu.reshape(-1, F), ffn_wd.reshape(-1, D)
    w = {"w_z": (a_w_in[0], 0, 0), "w_xbc": (a_w_in[0], 0, d_inner), "w_dt": (a_w_in[0], 0, d_zxbc),
         "w_out": (a_w_out[0], 0, 0),
         "w_k": (w_kvf, 0, 0), "w_v": (w_kvf, 0, D), "w_q": (b_wq[0], 0, 0), "w_o": (b_wo[0], 0, 0)}
    for idx in range(wg.shape[0] // D):
        w[f"wgu{idx}"], w[f"wd{idx}"] = ((wg, idx, 0), (wu, idx, 0)), (wd, idx, 0)

    bp = x_prompt.shape[0]
    zero_conv = jnp.zeros((state_conv.shape[0], bp) + state_conv.shape[2:], state_conv.dtype)
    zero_ssm = jnp.zeros((state_ssm.shape[0], bp) + state_ssm.shape[2:], state_ssm.dtype)
    empty_kv = jnp.zeros((bp, 0) + cache_k.shape[2:], cache_k.dtype)
    empty_lf = jnp.zeros((bp, 0, heads), cache_logf.dtype)
    (y_s, s_conv, s_ssm, s_k, s_v, s_logf), wb = _trunk(
        x_sample, state_conv, state_ssm, cache_k, cache_v, cache_logf, w, p)
    (y_p, p_conv, p_ssm, p_k, p_v, p_logf), _ = _trunk(
        x_prompt, zero_conv, zero_ssm, empty_kv, empty_kv, empty_lf, wb, p)
    return (y_p, y_s, p_conv, p_ssm, p_k, p_v, p_logf, s_conv, s_ssm, s_k, s_v, s_logf)
```

```python
import functools

import numpy as np
import jax
import jax.numpy as jnp
from jax import lax
from jax.experimental import pallas as pl
from jax.experimental.pallas import tpu as pltpu

NORM_EPS = 1e-5
SSD_CHUNK = 64
SSM_HEAD_DIM = 64
CONV_W = 4
LANES = 128
LOG2E = 1.4426950408889634
VMEM_LIMIT_BYTES = 56 * 1024 * 1024
CONV_VMEM_LIMIT_BYTES = 60 * 1024 * 1024
SSD_GROUPS_PER_STEP = 8
GATEUP_ROW_CHUNK = 512
CONV_COL_CHUNK = 256
ATTN_BLOCK = 256
ATTN_HEADS_PER_STEP = 4
BF16 = jnp.bfloat16
F32 = jnp.float32


def _params(*sem, vmem=VMEM_LIMIT_BYTES):
    return pltpu.CompilerParams(dimension_semantics=sem, vmem_limit_bytes=vmem)


def _tile(n, pref):
    if n <= pref:
        return n
    t = 1 << (pref.bit_length() - 1)
    while n % t:
        t //= 2
    return t


def _sigmoid(x):
    return 1.0 / (1.0 + jnp.exp(-x))


def _silu(x):
    return x * _sigmoid(x)


def _softplus(x):
    return jnp.maximum(x, 0.0) + jnp.log1p(jnp.exp(-jnp.abs(x)))


def _cumsum_rows(x, seg):
    rowi = jnp.bitwise_and(lax.broadcasted_iota(jnp.int32, x.shape, 0), seg - 1)
    s = 1
    while s < seg:
        x = x + jnp.where(rowi >= s, pltpu.roll(x, s, 0), 0.0)
        s *= 2
    return x


def _rmsnorm_kernel(x_ref, g_ref, o_ref):
    x = x_ref[...]
    y = x * lax.rsqrt(jnp.mean(x * x, axis=-1, keepdims=True) + NORM_EPS)
    o_ref[...] = (y * g_ref[...]).astype(o_ref.dtype)


def rmsnorm(x, g, out_dtype):
    M, D = x.shape
    tm = _tile(M, 512)
    return pl.pallas_call(
        _rmsnorm_kernel,
        grid=(M // tm,),
        in_specs=[pl.BlockSpec((tm, D), lambda i: (i, 0)),
                  pl.BlockSpec((1, D), lambda i: (0, 0))],
        out_specs=pl.BlockSpec((tm, D), lambda i: (i, 0)),
        out_shape=jax.ShapeDtypeStruct((M, D), out_dtype),
        compiler_params=_params("parallel"),
        name="rmsnorm",
    )(x, g.reshape(1, D))


def _fold_lanes(v):
    parts = [v[:, k * LANES:(k + 1) * LANES] for k in range(v.shape[1] // LANES)]
    while len(parts) > 1:
        parts = [a + b for a, b in zip(parts[::2], parts[1::2])] + parts[len(parts) & ~1:]
    return parts[0]


def _row_scale(ssq, d):
    r = lax.rsqrt(jnp.sum(ssq, axis=-1, keepdims=True) * (1.0 / d) + NORM_EPS)
    return jnp.broadcast_to(r, ssq.shape)


def _scale_rows(acc, rs_ref):
    return acc * jnp.concatenate([rs_ref[...]] * (acc.shape[1] // LANES), axis=1)


def _norm_prep_kernel(x_ref, g_ref, hb_ref, rs_ref):
    x = x_ref[...]
    hb_ref[...] = (x * g_ref[...]).astype(hb_ref.dtype)
    rs_ref[...] = _row_scale(_fold_lanes(x * x), x.shape[1])


def norm_prep(x, g):
    M, D = x.shape
    tm = _tile(M, 512)
    hb, rs = pl.pallas_call(
        _norm_prep_kernel,
        grid=(M // tm,),
        in_specs=[pl.BlockSpec((tm, D), lambda i: (i, 0)),
                  pl.BlockSpec((1, D), lambda i: (0, 0))],
        out_specs=[pl.BlockSpec((tm, D), lambda i: (i, 0)),
                   pl.BlockSpec((tm, LANES), lambda i: (i, 0))],
        out_shape=[jax.ShapeDtypeStruct((M, D), BF16), jax.ShapeDtypeStruct((M, LANES), F32)],
        compiler_params=_params("parallel"),
        name="norm_prep",
    )(x, g.reshape(1, D))
    return hb, rs


def _w_bf16(w_ref, wb_ref):
    w = w_ref[...]
    if wb_ref is not None:
        w = w.astype(BF16)
        wb_ref[...] = w
    return w


def _w_specs(w, K, n_cols, tn, M, tm):
    arr, row_blk, col_off = w
    emit = arr.dtype != BF16
    assert col_off % tn == 0 and n_cols % tn == 0
    assert not emit or M == tm
    cb = col_off // tn
    in_spec = pl.BlockSpec((K, tn), lambda i, j: (row_blk, cb + j))
    out_specs = [pl.BlockSpec((K, tn), lambda i, j: (0, j))] if emit else []
    out_shapes = [jax.ShapeDtypeStruct((K, n_cols), BF16)] if emit else []
    return arr, emit, in_spec, out_specs, out_shapes


def _x_specs(xn, tm):
    hb, _ = xn
    return [pl.BlockSpec((tm, hb.shape[1]), lambda i, j: (i, 0)),
            pl.BlockSpec((tm, LANES), lambda i, j: (i, 0))]


def _mm_plain_kernel(x_ref, rs_ref, w_ref, *o_refs, emit, silu):
    wb_ref = o_refs[-1] if emit else None
    acc = jnp.dot(x_ref[...], _w_bf16(w_ref, wb_ref), preferred_element_type=F32)
    acc = _scale_rows(acc, rs_ref)
    if silu:
        acc = _silu(acc)
    for o_ref in o_refs[:len(o_refs) - emit]:
        o_ref[...] = acc.astype(o_ref.dtype)


def mm_plain(xn, w, out_dtypes, *, n_cols, tm=1024, tn=1024, silu=False, name="mm_plain"):
    M, K = xn[0].shape
    tm, tn = _tile(M, tm), _tile(n_cols, tn)
    arr, emit, w_spec, wb_specs, wb_shapes = _w_specs(w, K, n_cols, tn, M, tm)
    outs = pl.pallas_call(
        functools.partial(_mm_plain_kernel, emit=emit, silu=silu),
        grid=(M // tm, n_cols // tn),
        in_specs=_x_specs(xn, tm) + [w_spec],
        out_specs=[pl.BlockSpec((tm, tn), lambda i, j: (i, j)) for _ in out_dtypes] + wb_specs,
        out_shape=[jax.ShapeDtypeStruct((M, n_cols), dt) for dt in out_dtypes] + wb_shapes,
        compiler_params=_params("parallel", "arbitrary"),
        name=name,
    )(*xn, arr)
    n = len(out_dtypes)
    return outs[:n], ((outs[n], 0, 0) if emit else w)


def _mm_conv_kernel(x_ref, rs_ref, w_ref, cw_ref, cb_ref, st_ref, o_ref, tail_ref, *rest,
                    emit, seg_len, tiles_per_seq):
    pad_scr, carry_scr = rest[-2:]
    i, j = pl.program_id(0), pl.program_id(1)
    tm, tn = o_ref.shape
    if emit:
        rest[0][...] = w_ref[...].astype(BF16)
        w_ref = rest[0]
    tail_ref[...] = jnp.zeros_like(tail_ref)

    def conv_silu(rows, cols):
        acc = cb_ref[:, cols] + pad_scr[5:5 + rows, cols] * cw_ref[0:1, cols]
        for k in range(1, CONV_W):
            acc = acc + pad_scr[5 + k:5 + k + rows, cols] * cw_ref[k:k + 1, cols]
        return _silu(acc)

    if seg_len == tm:
        prev = st_ref[0]
        if tiles_per_seq > 1:
            @pl.when(i == 0)
            def _():
                carry_scr[j] = jnp.zeros(carry_scr.shape[1:], F32)

            prev = jnp.where(i % tiles_per_seq == 0, prev, carry_scr[j, 0:CONV_W - 1, :])
        pad_scr[5:8, :] = prev
        x = x_ref[...]
        rs = jnp.concatenate([rs_ref[...]] * (CONV_COL_CHUNK // LANES), axis=1)
        for c0 in range(0, tn, CONV_COL_CHUNK):
            cols = slice(c0, c0 + CONV_COL_CHUNK)
            pad_scr[8:8 + tm, cols] = jnp.dot(x, w_ref[:, cols], preferred_element_type=F32) * rs
            o_ref[:, cols] = conv_silu(tm, cols)
        last = pad_scr[5 + tm:8 + tm, :]
        tail_ref[0, 0:CONV_W - 1, :] = last
        if tiles_per_seq > 1:
            carry_scr[j, 0:CONV_W - 1, :] = last
    else:
        raw = _scale_rows(jnp.dot(x_ref[...], w_ref[...], preferred_element_type=F32), rs_ref)
        for s in range(tm // seg_len):
            pad_scr[5:8, :] = st_ref[s]
            pad_scr[8:8 + seg_len, :] = raw[s * seg_len:(s + 1) * seg_len]
            o_ref[s * seg_len:(s + 1) * seg_len, :] = conv_silu(seg_len, slice(0, tn))
            tail_ref[s, 0:CONV_W - 1, :] = pad_scr[5 + seg_len:8 + seg_len, :]


def mm_conv(xn, w, conv_w, conv_b, conv_state, *, T, n_cols, tm=1024, tn=1024):
    M, K = xn[0].shape
    Bsz = M // T
    tm, tn = _tile(M, tm), _tile(n_cols, tn)
    seg_len = min(T, tm)
    assert tm % seg_len == 0 and T % seg_len == 0 and seg_len >= CONV_W - 1
    nseg, tiles_per_seq = tm // seg_len, T // seg_len
    arr, emit, w_spec, wb_specs, wb_shapes = _w_specs(w, K, n_cols, tn, M, tm)
    outs = pl.pallas_call(
        functools.partial(_mm_conv_kernel, emit=emit, seg_len=seg_len, tiles_per_seq=tiles_per_seq),
        grid=(M // tm, n_cols // tn),
        in_specs=_x_specs(xn, tm) + [
            w_spec,
            pl.BlockSpec((CONV_W, tn), lambda i, j: (0, j)),
            pl.BlockSpec((1, tn), lambda i, j: (0, j)),
            pl.BlockSpec((nseg, CONV_W - 1, tn), lambda i, j: (i // tiles_per_seq, 0, j))],
        out_specs=[pl.BlockSpec((tm, tn), lambda i, j: (i, j)),
                   pl.BlockSpec((nseg, 8, tn), lambda i, j: (i, 0, j))] + wb_specs,
        out_shape=[jax.ShapeDtypeStruct((M, n_cols), F32),
                   jax.ShapeDtypeStruct((M // seg_len, 8, n_cols), F32)] + wb_shapes,
        scratch_shapes=[pltpu.VMEM((8 + tm if nseg == 1 else 8 + seg_len, tn), F32),
                        pltpu.VMEM((n_cols // tn, 8, tn), F32)],
        compiler_params=_params("arbitrary", "arbitrary", vmem=CONV_VMEM_LIMIT_BYTES),
        name="mm_conv",
    )(*xn, arr, conv_w, conv_b.reshape(1, n_cols), conv_state)
    tails = outs[1].reshape(Bsz, tiles_per_seq, 8, n_cols)[:, -1, 0:CONV_W - 1, :]
    return outs[0], tails, ((outs[2], 0, 0) if emit else w)


def _mm_resid_kernel(*refs, alpha, n_norm, emit, n_total):
    x_ref, w_ref, r_ref = refs[:3]
    g_refs = refs[3:3 + n_norm]
    o_ref = refs[3 + n_norm]
    hb_refs = refs[4 + n_norm:4 + 2 * n_norm]
    rs_ref = refs[4 + 2 * n_norm] if n_norm else None
    wb_ref = refs[-1] if emit else None
    acc = jnp.dot(x_ref[...], _w_bf16(w_ref, wb_ref), preferred_element_type=F32)
    h = r_ref[...] + alpha * acc
    o_ref[...] = h
    for g_ref, hb_ref in zip(g_refs, hb_refs):
        hb_ref[...] = (h * g_ref[...]).astype(hb_ref.dtype)
    if n_norm:
        part = _fold_lanes(h * h)
        j = pl.program_id(1)

        @pl.when(j == 0)
        def _():
            rs_ref[...] = part

        @pl.when(j > 0)
        def _():
            rs_ref[...] += part

        @pl.when(j == pl.num_programs(1) - 1)
        def _():
            rs_ref[...] = _row_scale(rs_ref[...], n_total)


def mm_resid(x, w, resid, alpha, gains=(), *, tm=512, tn=512, name="mm_resid"):
    M, K = x.shape
    N = resid.shape[1]
    tm, tn = _tile(M, tm), _tile(N, tn)
    n_norm = len(gains)
    arr, emit, w_spec, wb_specs, wb_shapes = _w_specs(w, K, N, tn, M, tm)
    tile = pl.BlockSpec((tm, tn), lambda i, j: (i, j))
    outs = pl.pallas_call(
        functools.partial(_mm_resid_kernel, alpha=alpha, n_norm=n_norm, emit=emit, n_total=N),
        grid=(M // tm, N // tn),
        in_specs=[pl.BlockSpec((tm, K), lambda i, j: (i, 0)), w_spec, tile]
                 + [pl.BlockSpec((1, tn), lambda i, j: (0, j))] * n_norm,
        out_specs=[tile] * (1 + n_norm) + [pl.BlockSpec((tm, LANES), lambda i, j: (i, 0))] * bool(n_norm) + wb_specs,
        out_shape=[jax.ShapeDtypeStruct((M, N), F32)] + [jax.ShapeDtypeStruct((M, N), BF16)] * n_norm
                  + [jax.ShapeDtypeStruct((M, LANES), F32)] * bool(n_norm) + wb_shapes,
        compiler_params=_params("parallel", "arbitrary"),
        name=name,
    )(x, arr, resid, *[g.reshape(1, N) for g in gains])
    xns = [(outs[1 + k], outs[1 + n_norm]) for k in range(n_norm)]
    return outs[0], xns, ((outs[-1], 0, 0) if emit else w)


def _gateup_epilogue(g, u, rs_ref, o_ref):
    o_ref[...] = (_silu(_scale_rows(g, rs_ref)) * _scale_rows(u, rs_ref)).astype(o_ref.dtype)


def _mm_gateup_pair_kernel(x_ref, rs_ref, wg_ref, wu_ref, o_ref, *wgu_refs):
    x = x_ref[...]
    wg, wu = wg_ref[...], wu_ref[...]
    if wgu_refs:
        tn = wg.shape[1]
        wg, wu = wg.astype(BF16), wu.astype(BF16)
        wgu_refs[0][:, 0:tn] = wg
        wgu_refs[0][:, tn:2 * tn] = wu
    g = jnp.dot(x, wg, preferred_element_type=F32)
    u = jnp.dot(x, wu, preferred_element_type=F32)
    _gateup_epilogue(g, u, rs_ref, o_ref)


def _mm_gateup_kernel(x_ref, rs_ref, wgu_ref, o_ref):
    tm, tn = o_ref.shape
    rows = min(tm, GATEUP_ROW_CHUNK)
    w = wgu_ref[...]
    for r0 in range(0, tm, rows):
        rsl = slice(r0, r0 + rows)
        gu = jnp.dot(x_ref[rsl, :], w, preferred_element_type=F32)
        rs = jnp.concatenate([rs_ref[rsl, :]] * (tn // LANES), axis=1)
        o_ref[rsl, :] = (_silu(gu[:, 0:tn] * rs) * (gu[:, tn:2 * tn] * rs)).astype(o_ref.dtype)


def mm_gateup(xn, wgu, *, n_cols, tm=2048, tn=256):
    M, K = xn[0].shape
    tm, tn = _tile(M, tm), _tile(n_cols, tn)
    out_spec = pl.BlockSpec((tm, tn), lambda i, j: (i, j))
    out_shape = jax.ShapeDtypeStruct((M, n_cols), BF16)
    if isinstance(wgu[0], tuple):
        garr, emit, g_spec, _, _ = _w_specs(wgu[0], K, n_cols, tn, M, tm)
        uarr, _, u_spec, _, _ = _w_specs(wgu[1], K, n_cols, tn, M, tm)
        outs = pl.pallas_call(
            _mm_gateup_pair_kernel,
            grid=(M // tm, n_cols // tn),
            in_specs=_x_specs(xn, tm) + [g_spec, u_spec],
            out_specs=[out_spec] + [pl.BlockSpec((K, 2 * tn), lambda i, j: (0, j))] * emit,
            out_shape=[out_shape] + [jax.ShapeDtypeStruct((K, 2 * n_cols), BF16)] * emit,
            compiler_params=_params("parallel", "arbitrary"),
            name="mm_gateup",
        )(*xn, garr, uarr)
        return outs[0], ((outs[1], 0, 0) if emit else wgu)
    arr, row_blk, col_off = wgu
    assert row_blk == 0 and col_off == 0 and arr.shape == (K, 2 * n_cols)
    out = pl.pallas_call(
        _mm_gateup_kernel,
        grid=(M // tm, n_cols // tn),
        in_specs=_x_specs(xn, tm) + [pl.BlockSpec((K, 2 * tn), lambda i, j: (0, j))],
        out_specs=out_spec,
        out_shape=out_shape,
        compiler_params=_params("parallel", "arbitrary"),
        name="mm_gateup",
    )(*xn, arr)
    return out, wgu


def _mm_dt_kernel(x_ref, rs_ref, w_ref, b_ref, alog_ref, dt_ref, acum_ref, *wb_refs, L):
    raw = jnp.dot(x_ref[...], _w_bf16(w_ref, wb_refs[0] if wb_refs else None), preferred_element_type=F32)
    dt = _softplus(_scale_rows(raw, rs_ref) + b_ref[...])
    dt_ref[...] = dt
    acum_ref[...] = _cumsum_rows(dt * (-jnp.exp(alog_ref[...])), L)


def mm_dt(xn, w, dt_bias, a_log, *, L, tm=2048):
    M, K = xn[0].shape
    heads = dt_bias.shape[0]
    tm = _tile(M, tm)
    assert tm % L == 0 and L & (L - 1) == 0
    arr, emit, w_spec, wb_specs, wb_shapes = _w_specs(w, K, heads, heads, M, tm)
    vec = pl.BlockSpec((1, heads), lambda i, j: (0, 0))
    outs = pl.pallas_call(
        functools.partial(_mm_dt_kernel, L=L),
        grid=(M // tm, 1),
        in_specs=_x_specs(xn, tm) + [w_spec, vec, vec],
        out_specs=[pl.BlockSpec((tm, heads), lambda i, j: (i, 0))] * 2 + wb_specs,
        out_shape=[jax.ShapeDtypeStruct((M, heads), F32)] * 2 + wb_shapes,
        compiler_params=_params("parallel", "arbitrary"),
        name="mm_dt",
    )(*xn, arr, dt_bias.reshape(1, heads), a_log.reshape(1, heads))
    return outs[0], outs[1], ((outs[2], 0, 0) if emit else w)


def _mm_logf_kernel(x_ref, rs_ref, w_ref, b_ref, o_ref):
    s = jnp.dot(x_ref[...], w_ref[...], preferred_element_type=F32)
    s = _scale_rows(s, rs_ref) + b_ref[...]
    o_ref[...] = -_softplus(-s)


def mm_logf(xn, w, b):
    M, K = xn[0].shape
    N = w.shape[1]
    tm = _tile(M, 2048)
    return pl.pallas_call(
        _mm_logf_kernel,
        grid=(M // tm, 1),
        in_specs=_x_specs(xn, tm) + [pl.BlockSpec((K, N), lambda i, j: (0, 0)),
                                     pl.BlockSpec((1, N), lambda i, j: (0, 0))],
        out_specs=pl.BlockSpec((tm, N), lambda i, j: (i, 0)),
        out_shape=jax.ShapeDtypeStruct((M, N), F32),
        compiler_params=_params("parallel", "arbitrary"),
        name="mm_logf",
    )(*xn, w, b)


def _split3(v):
    hi = v.astype(BF16)
    r1 = v - hi.astype(F32)
    mid = r1.astype(BF16)
    lo = (r1 - mid.astype(F32)).astype(BF16)
    return [hi, mid, lo]


def _ssd_kernel(gate_ref, x_ref, b_ref, c_ref, dt_ref, acum_ref, dx_ref, ng_ref, e_ref, h0_ref,
                y_ref, hout_ref, ht_scr, *, L, GW):
    c = pl.program_id(2)
    P = SSM_HEAD_DIM
    N = LANES
    n_sub = x_ref.shape[1] // GW

    @pl.when(c == 0)
    def _():
        for s in range(n_sub):
            ht_scr[s] = h0_ref[0, s * GW:(s + 1) * GW, :].T

    lhs = jnp.concatenate(_split3(acum_ref[...]) + _split3(dt_ref[...]), axis=0)

    lane = lax.broadcasted_iota(jnp.int32, (L, LANES), 1)
    rowi = lax.broadcasted_iota(jnp.int32, (L, LANES), 0)
    s_idx = jnp.bitwise_and(lane, P - 1)
    r_idx = jnp.right_shift(lane, P.bit_length() - 1)
    eye2 = rowi == s_idx
    tri2 = s_idx <= rowi
    pad = [] if L == P else [jnp.zeros((P - L, LANES), BF16)]

    def early_dots(s):
        Bb = b_ref[:, s * N:(s + 1) * N].astype(BF16)
        Cb = c_ref[:, s * N:(s + 1) * N].astype(BF16)
        ht = ht_scr[s]
        ex = jnp.dot(lhs, e_ref[s], preferred_element_type=F32)
        ch = jnp.dot(Cb, ht.astype(BF16), preferred_element_type=F32)
        b2 = jnp.concatenate([Bb] + pad + [Bb] + pad, axis=0)
        cb2 = lax.dot_general(Cb, b2, (((1,), (1,)), ((), ())), preferred_element_type=F32)
        return Bb, ht, ex, ch, cb2

    early = {0: early_dots(0)}
    for s in range(n_sub):
        if s + 1 < n_sub:
            early[s + 1] = early_dots(s + 1)
        Bb, ht, ex, ch, cb2 = early.pop(s)
        xs = x_ref[:, s * GW:(s + 1) * GW]
        acum_x = ex[0:L] + ex[L:2 * L] + ex[2 * L:3 * L]
        dt_x = ex[3 * L:4 * L] + ex[4 * L:5 * L] + ex[5 * L:6 * L]

        xdt = xs * dt_x
        y_inter = ch * jnp.exp(acum_x)

        y_parts = []
        for j in range(GW // LANES):
            sl = slice(j * LANES, (j + 1) * LANES)
            ax = acum_x[:, sl]
            v = jnp.sum(jnp.where(eye2, ax, 0.0), axis=0, keepdims=True)
            decay = jnp.exp(jnp.where(tri2, ax - v, -jnp.inf))
            m2 = (cb2 * decay).astype(BF16)
            xp = xdt[:, sl]
            top = jnp.where(r_idx == 0, xp, 0.0).astype(BF16)
            bot = jnp.where(r_idx == 1, xp, 0.0).astype(BF16)
            bd = jnp.concatenate([top] + pad + [bot] + pad, axis=0)
            y_parts.append(jnp.dot(m2, bd, preferred_element_type=F32))
        y = jnp.concatenate(y_parts, axis=1) + y_inter + dx_ref[:, s * GW:(s + 1) * GW] * xs

        y = y * gate_ref[:, s * GW:(s + 1) * GW]
        y = y * lax.rsqrt(jnp.mean(y * y, axis=-1, keepdims=True) + NORM_EPS)
        y_ref[:, s * GW:(s + 1) * GW] = (y * ng_ref[:, s * GW:(s + 1) * GW]).astype(y_ref.dtype)

        last = acum_x[L - 1:L, :]
        xt = (xdt * jnp.exp(last - acum_x)).astype(BF16)
        upd = lax.dot_general(Bb, xt, (((0,), (0,)), ((), ())), preferred_element_type=F32)
        ht_scr[s] = ht * jnp.exp(last) + upd

    @pl.when(c == pl.num_programs(2) - 1)
    def _():
        for s in range(n_sub):
            hout_ref[0, s * GW:(s + 1) * GW, :] = ht_scr[s].T


def ssd_mixer(gate, xbc, dt, acum, ssm_state, d_skip, norm_g, *, Bsz, T, L, d_inner, groups, n_state):
    M = Bsz * T
    heads = dt.shape[1]
    P = SSM_HEAD_DIM
    GW = d_inner // groups
    N = n_state
    R = GW // P
    nc = T // L
    ns = SSD_GROUPS_PER_STEP
    assert N == LANES and heads == LANES and GW % LANES == 0 and L % 16 == 0 and L <= P and groups % ns == 0
    SW, SN = ns * GW, ns * N
    kb, kc = d_inner // SN, (d_inner + groups * N) // SN

    expand = np.zeros((groups, heads, GW), np.float32)
    for g in range(groups):
        for r in range(R):
            expand[g, g * R + r, r * P:(r + 1) * P] = 1.0
    expand = jnp.asarray(expand, BF16)

    d_x = jnp.repeat(d_skip, P).reshape(1, d_inner)
    h0 = ssm_state.reshape(Bsz, heads * P, N)

    row = lambda b, g, c: b * nc + c
    in_specs = [
        pl.BlockSpec((L, SW), lambda b, g, c: (row(b, g, c), g)),
        pl.BlockSpec((L, SW), lambda b, g, c: (row(b, g, c), g)),
        pl.BlockSpec((L, SN), lambda b, g, c: (row(b, g, c), kb + g)),
        pl.BlockSpec((L, SN), lambda b, g, c: (row(b, g, c), kc + g)),
        pl.BlockSpec((L, heads), lambda b, g, c: (row(b, g, c), 0)),
        pl.BlockSpec((L, heads), lambda b, g, c: (row(b, g, c), 0)),
        pl.BlockSpec((1, SW), lambda b, g, c: (0, g)),
        pl.BlockSpec((1, SW), lambda b, g, c: (0, g)),
        pl.BlockSpec((ns, heads, GW), lambda b, g, c: (g, 0, 0)),
        pl.BlockSpec((1, SW, N), lambda b, g, c: (b, g, 0)),
    ]
    out_specs = [
        pl.BlockSpec((L, SW), lambda b, g, c: (row(b, g, c), g)),
        pl.BlockSpec((1, SW, N), lambda b, g, c: (b, g, 0)),
    ]
    out_shape = [
        jax.ShapeDtypeStruct((M, d_inner), BF16),
        jax.ShapeDtypeStruct((Bsz, heads * P, N), F32),
    ]
    y, h_new = pl.pallas_call(
        functools.partial(_ssd_kernel, L=L, GW=GW),
        grid=(Bsz, groups // ns, nc),
        in_specs=in_specs,
        out_specs=out_specs,
        out_shape=out_shape,
        scratch_shapes=[pltpu.VMEM((ns, N, GW), F32)],
        compiler_params=_params("parallel", "parallel", "arbitrary"),
        name="ssd_mixer",
    )(gate, xbc, xbc, xbc, dt, acum, d_x, norm_g.reshape(1, d_inner), expand, h0)
    return y, h_new.reshape(Bsz, heads, P, N)


def _fcum_kernel(past_ref, new_ref, o_ref, w_scr, *, past, Tn, heads):
    w_scr[...] = jnp.zeros_like(w_scr)
    if past:
        w_scr[0:past, 0:heads] = past_ref[0]
    w_scr[past:past + Tn, :] = new_ref[...]
    n = w_scr.shape[0]
    o_ref[0] = _cumsum_rows(w_scr[...], 1 << (n - 1).bit_length()).T


def fcum(cache_logf, logf_new, *, Bsz, Tn):
    past, heads = cache_logf.shape[1], cache_logf.shape[2]
    t_pad = -(-(past + Tn) // LANES) * LANES
    if past == 0:
        cache_logf = jnp.zeros((Bsz, 8, heads), F32)
    return pl.pallas_call(
        functools.partial(_fcum_kernel, past=past, Tn=Tn, heads=heads),
        grid=(Bsz,),
        in_specs=[pl.BlockSpec((1,) + cache_logf.shape[1:], lambda b: (b, 0, 0)),
                  pl.BlockSpec((Tn, LANES), lambda b: (b, 0))],
        out_specs=pl.BlockSpec((1, LANES, t_pad), lambda b: (b, 0, 0)),
        out_shape=jax.ShapeDtypeStruct((Bsz, LANES, t_pad), F32),
        scratch_shapes=[pltpu.VMEM((t_pad, LANES), F32)],
        compiler_params=_params("parallel"),
        name="fcum",
    )(cache_logf, logf_new)


def _row_to_cols(frow):
    return jnp.broadcast_to(frow, (LANES, frow.shape[1])).T


def _attn_prompt_kernel(q_ref, k_ref, v_ref, f_ref, o_ref, vt_scr, fk_scr, *, tb, hd, scale):
    T = q_ref.shape[0]
    nh = q_ref.shape[1] // hd
    nb = T // tb
    h0 = (pl.program_id(1) % (8 // nh)) * nh
    scale = scale * LOG2E
    frows = [f_ref[0, pl.ds(h0 + i, 1), :] * LOG2E for i in range(nh)]
    wide = 2 * tb
    for i in range(nh):
        fk_scr[i] = _row_to_cols(frows[i])
        for j in range(T // wide):
            vt_scr[i, j] = v_ref[j * wide:(j + 1) * wide, i * hd:(i + 1) * hd].astype(F32).T.astype(BF16)
    kofs = lax.broadcasted_iota(jnp.int32, (tb, tb), 0)
    qofs = lax.broadcasted_iota(jnp.int32, (tb, tb), 1)

    for qi in range(nb):
        qs = slice(qi * tb, (qi + 1) * tb)
        q_t = [q_ref[qs, i * hd:(i + 1) * hd].astype(F32).T.astype(BF16) for i in range(nh)]

        def block(start, width, vt, carry, masked):
            qk = [jnp.dot(k_ref[pl.ds(start, width), i * hd:(i + 1) * hd], q_t[i], preferred_element_type=F32)
                  for i in range(nh)]
            out = []
            for i in range(nh):
                m, l, acc = carry[3 * i:3 * i + 3]
                fk = fk_scr[i, pl.ds(start, width), :]
                s = qk[i] * scale
                s = s + frows[i][:, qs] - jnp.concatenate([fk] * (tb // LANES), axis=1)
                if masked:
                    s = jnp.where(kofs <= qofs, s, -jnp.inf)
                m_new = jnp.maximum(m, jnp.max(s, axis=0, keepdims=True))
                alpha = jnp.exp2(m - m_new)
                p = jnp.exp2(s - m_new)
                l = alpha * l + jnp.sum(p, axis=0, keepdims=True)
                acc = alpha * acc + jnp.dot(vt(i), p.astype(BF16), preferred_element_type=F32)
                out += [m_new, l, acc]
            return tuple(out)

        def wide_block(jw, carry):
            return block(jw * wide, wide, lambda i: vt_scr[i, jw], carry, False)

        def narrow_block(j, carry, masked):
            half = slice((j % 2) * tb, (j % 2 + 1) * tb)
            return block(j * tb, tb, lambda i: vt_scr[i, j // 2, :, half], carry, masked)

        carry = (jnp.full((1, tb), -jnp.inf, F32), jnp.zeros((1, tb), F32), jnp.zeros((hd, tb), F32)) * nh
        for jw in range(qi // 2):
            carry = wide_block(jw, carry)
        if qi % 2:
            carry = narrow_block(qi - 1, carry, False)
        carry = narrow_block(qi, carry, True)
        for i in range(nh):
            o_ref[qs, i * hd:(i + 1) * hd] = (carry[3 * i + 2] / carry[3 * i + 1]).T.astype(o_ref.dtype)


def attn_prompt(q, k, v, f_t, *, Bsz, T, heads, hd):
    tb = _tile(T, ATTN_BLOCK)
    nh = ATTN_HEADS_PER_STEP
    assert tb % LANES == 0 and T % (2 * tb) == 0 and heads % 8 == 0 and 8 % nh == 0 and hd == LANES
    blk = pl.BlockSpec((T, nh * hd), lambda b, h: (b, h))
    return pl.pallas_call(
        functools.partial(_attn_prompt_kernel, tb=tb, hd=hd, scale=hd ** -0.5),
        grid=(Bsz, heads // nh),
        in_specs=[blk, blk, blk, pl.BlockSpec((1, 8, T), lambda b, h: (b, h * nh // 8, 0))],
        out_specs=blk,
        out_shape=jax.ShapeDtypeStruct((Bsz * T, heads * hd), BF16),
        scratch_shapes=[pltpu.VMEM((nh, T // (2 * tb), hd, 2 * tb), BF16), pltpu.VMEM((nh, T, LANES), F32)],
        compiler_params=_params("parallel", "parallel"),
        name="attn_prompt",
    )(q, k, v, f_t)


def _attn_sample_kernel(q_ref, kc_ref, vc_ref, kn_ref, vn_ref, f_ref, o_ref, *, Tn, past, hd, nh, scale):
    zrows = jnp.zeros((LANES - Tn, hd), BF16)
    rowi = lax.broadcasted_iota(jnp.int32, (Tn, LANES), 0)
    coli = lax.broadcasted_iota(jnp.int32, (Tn, LANES), 1)
    nt = (((1,), (1,)), ((), ()))
    for hh in range(nh):
        sl = slice(hh * hd, (hh + 1) * hd)
        q = q_ref[:, sl]
        kc = kc_ref[0, :, sl].astype(BF16)
        vc = vc_ref[0, :, sl].astype(BF16)
        kn = jnp.concatenate([kn_ref[:, sl], zrows], axis=0)
        vn = jnp.concatenate([vn_ref[:, sl], zrows], axis=0)
        f_past = f_ref[0, hh:hh + 1, 0:past]
        f_new = f_ref[0, hh:hh + 1, past:past + LANES]
        fq_col = _row_to_cols(f_new)[0:Tn, 0:1]
        s1 = lax.dot_general(q, kc, nt, preferred_element_type=F32) * scale + fq_col - f_past
        s2 = lax.dot_general(q, kn, nt, preferred_element_type=F32) * scale + fq_col - f_new
        s2 = jnp.where(coli <= rowi, s2, -jnp.inf)
        m = jnp.maximum(jnp.max(s1, axis=1, keepdims=True), jnp.max(s2, axis=1, keepdims=True))
        p1 = jnp.exp(s1 - m)
        p2 = jnp.exp(s2 - m)
        l = jnp.sum(p1, axis=1, keepdims=True) + jnp.sum(p2, axis=1, keepdims=True)
        o = jnp.dot(p1.astype(BF16), vc, preferred_element_type=F32)
        o = o + jnp.dot(p2.astype(BF16), vn, preferred_element_type=F32)
        o_ref[:, sl] = (o / l).astype(o_ref.dtype)


def attn_sample(q, cache_k, cache_v, k_new, v_new, f_t, *, Bsz, Tn, heads, hd):
    past = cache_k.shape[1]
    nh = 8
    assert past % LANES == 0 and Tn % 16 == 0 and Tn <= LANES and heads % nh == 0
    ck = cache_k.reshape(Bsz, past, heads * hd)
    cv = cache_v.reshape(Bsz, past, heads * hd)
    return pl.pallas_call(
        functools.partial(_attn_sample_kernel, Tn=Tn, past=past, hd=hd, nh=nh, scale=hd ** -0.5),
        grid=(Bsz, heads // nh),
        in_specs=[pl.BlockSpec((Tn, nh * hd), lambda b, g: (b, g)),
                  pl.BlockSpec((1, past, nh * hd), lambda b, g: (b, 0, g)),
                  pl.BlockSpec((1, past, nh * hd), lambda b, g: (b, 0, g)),
                  pl.BlockSpec((Tn, nh * hd), lambda b, g: (b, g)),
                  pl.BlockSpec((Tn, nh * hd), lambda b, g: (b, g)),
                  pl.BlockSpec((1, nh, f_t.shape[2]), lambda b, g: (b, g, 0))],
        out_specs=pl.BlockSpec((Tn, nh * hd), lambda b, g: (b, g)),
        out_shape=jax.ShapeDtypeStruct((Bsz * Tn, heads * hd), BF16),
        compiler_params=_params("parallel", "parallel"),
        name="attn_sample",
    )(q, ck, cv, k_new, v_new, f_t)


def _trunk(x, conv_st, ssm_st, cache_k, cache_v, cache_logf, w, p):
    Bsz, T, D = x.shape
    M = Bsz * T
    heads, hd = cache_k.shape[2], cache_k.shape[3]
    d_inner = p["a_norm"].shape[-1]
    d_ff = p["d_ff"]
    n_state = ssm_st.shape[-1]
    groups = (conv_st.shape[-1] - d_inner) // (2 * n_state)
    d_conv = d_inner + 2 * groups * n_state
    L = SSD_CHUNK if T % SSD_CHUNK == 0 else T
    small = M <= 512
    wb = {}

    norm_ffn = p["norm_ffn"].reshape(-1, D)
    tn_r = 256 if small else 512

    def ffn(h, xn, idx, next_gains):
        a, wb[f"wgu{idx}"] = mm_gateup(xn, w[f"wgu{idx}"], n_cols=d_ff)
        h, xns, wb[f"wd{idx}"] = mm_resid(a, w[f"wd{idx}"], h, 0.5, next_gains, tn=tn_r, name="mm_down")
        return h, xns

    h = x.reshape(M, D)
    h, (u,) = ffn(h, norm_prep(h, norm_ffn[0]), 0, [p["norm_mix"][0]])
    tn_in = 512 if small else 1024
    (gate,), wb["w_z"] = mm_plain(u, w["w_z"], [F32], n_cols=d_inner, tn=tn_in, silu=True, name="mm_z")
    xbc, new_conv, wb["w_xbc"] = mm_conv(u, w["w_xbc"], p["a_conv_w"][0], p["a_conv_b"][0], conv_st[0],
                                         T=T, n_cols=d_conv, tn=tn_in)
    dt, acum, wb["w_dt"] = mm_dt(u, w["w_dt"], p["a_dt_bias"][0], p["a_A_log"][0], L=L)
    y, new_ssm = ssd_mixer(gate, xbc, dt, acum, ssm_st[0], p["a_D"][0], p["a_norm"][0],
                           Bsz=Bsz, T=T, L=L, d_inner=d_inner, groups=groups, n_state=n_state)
    h, (xn,), wb["w_out"] = mm_resid(y, w["w_out"], h, 1.0, [norm_ffn[1]], tm=1024, tn=256, name="mm_out_proj")
    h, (x_kv, xn) = ffn(h, xn, 1, [p["kv_norm"], norm_ffn[2]])
    tn = 512 if small else 1024
    logf = mm_logf(x_kv, p["w_f"], p["b_f"])
    f_t = fcum(cache_logf, logf, Bsz=Bsz, Tn=T)
    (k32, kb), wb["w_k"] = mm_plain(x_kv, w["w_k"], [F32, BF16], n_cols=D, tn=tn, name="mm_k")
    (v32, vb), wb["w_v"] = mm_plain(x_kv, w["w_v"], [F32, BF16], n_cols=D, tn=tn, name="mm_v")
    h, (u,) = ffn(h, xn, 2, [p["norm_mix"][1]])
    (q,), wb["w_q"] = mm_plain(u, w["w_q"], [BF16], n_cols=D, tn=tn, name="mm_q")
    if cache_k.shape[1] == 0:
        o = attn_prompt(q, kb, vb, f_t, Bsz=Bsz, T=T, heads=heads, hd=hd)
    else:
        o = attn_sample(q, cache_k, cache_v, kb, vb, f_t, Bsz=Bsz, Tn=T, heads=heads, hd=hd)
    h, (xn,), wb["w_o"] = mm_resid(o, w["w_o"], h, 1.0, [norm_ffn[3]], tm=1024, tn=512, name="mm_wo")
    h, _ = ffn(h, xn, 3, [])
    y_out = rmsnorm(h, p["norm_final"], F32).reshape(Bsz, T, D)
    outs = (y_out, new_conv[None], new_ssm[None],
            k32.reshape(Bsz, T, heads, hd), v32.reshape(Bsz, T, heads, hd),
            logf[:, :heads].reshape(Bsz, T, heads))
    return outs, wb


def kernel(x_prompt, x_sample, state_conv, state_ssm, cache_k, cache_v, cache_logf, norm_ffn, ffn_wg, ffn_wu,
           ffn_wd, norm_mix, a_w_in, a_conv_w, a_conv_b, a_dt_bias, a_A_log, a_D, a_norm, a_w_out, kv_norm,
           w_kvf, b_f, b_wq, b_wo, norm_final):
    assert norm_mix.shape[0] == 2 and a_w_in.shape[0] == 1 and b_wq.shape[0] == 1
    D = x_prompt.shape[-1]
    F = ffn_wg.shape[-1]
    heads = cache_k.shape[2]
    d_inner = a_norm.shape[-1]
    d_zxbc = a_w_in.shape[-1] - a_dt_bias.shape[-1]
    p = {"norm_ffn": norm_ffn, "norm_mix": norm_mix, "a_conv_w": a_conv_w, "a_conv_b": a_conv_b,
         "a_dt_bias": a_dt_bias, "a_A_log": a_A_log, "a_D": a_D, "a_norm": a_norm, "kv_norm": kv_norm,
         "norm_final": norm_final, "d_ff": F,
         "w_f": jnp.pad(w_kvf[:, 2 * D:], ((0, 0), (0, LANES - heads))).astype(BF16),
         "b_f": jnp.pad(b_f, (0, LANES - heads)).reshape(1, LANES)}
    wg, wu, wd = ffn_wg.reshape(-1, F), ffn_wu.reshape(-1, F), ffn_wd.reshape(-1, D)
    w = {"w_z": (a_w_in[0], 0, 0), "w_xbc": (a_w_in[0], 0, d_inner), "w_dt": (a_w_in[0], 0, d_zxbc),
         "w_out": (a_w_out[0], 0, 0),
         "w_k": (w_kvf, 0, 0), "w_v": (w_kvf, 0, D), "w_q": (b_wq[0], 0, 0), "w_o": (b_wo[0], 0, 0)}
    for idx in range(wg.shape[0] // D):
        w[f"wgu{idx}"], w[f"wd{idx}"] = ((wg, idx, 0), (wu, idx, 0)), (wd, idx, 0)

    bp = x_prompt.shape[0]
    zero_conv = jnp.zeros((state_conv.shape[0], bp) + state_conv.shape[2:], state_conv.dtype)
    zero_ssm = jnp.zeros((state_ssm.shape[0], bp) + state_ssm.shape[2:], state_ssm.dtype)
    empty_kv = jnp.zeros((bp, 0) + cache_k.shape[2:], cache_k.dtype)
    empty_lf = jnp.zeros((bp, 0, heads), cache_logf.dtype)
    (y_s, s_conv, s_ssm, s_k, s_v, s_logf), wb = _trunk(
        x_sample, state_conv, state_ssm, cache_k, cache_v, cache_logf, w, p)
    (y_p, p_conv, p_ssm, p_k, p_v, p_logf), _ = _trunk(
        x_prompt, zero_conv, zero_ssm, empty_kv, empty_kv, empty_lf, wb, p)
    return (y_p, y_s, p_conv, p_ssm, p_k, p_v, p_logf, s_conv, s_ssm, s_k, s_v, s_logf)
```

```python
import functools

import numpy as np
import jax
import jax.numpy as jnp
from jax import lax
from jax.experimental import pallas as pl
from jax.experimental.pallas import tpu as pltpu

NORM_EPS = 1e-5
SSD_CHUNK = 64
SSM_HEAD_DIM = 64
CONV_W = 4
LANES = 128
LOG2E = 1.4426950408889634
VMEM_LIMIT_BYTES = 56 * 1024 * 1024
CONV_VMEM_LIMIT_BYTES = 60 * 1024 * 1024
SSD_GROUPS_PER_STEP = 8
W_STREAM_BUFFERS = 3
GATEUP_ROW_CHUNK = 512
CONV_COL_CHUNK = 256
ATTN_BLOCK = 256
ATTN_HEADS_PER_STEP = 4
BF16 = jnp.bfloat16
F32 = jnp.float32


def _params(*sem, vmem=VMEM_LIMIT_BYTES):
    return pltpu.CompilerParams(dimension_semantics=sem, vmem_limit_bytes=vmem)


def _tile(n, pref):
    if n <= pref:
        return n
    t = 1 << (pref.bit_length() - 1)
    while n % t:
        t //= 2
    return t


def _sigmoid(x):
    return 1.0 / (1.0 + jnp.exp(-x))


def _silu(x):
    return x * _sigmoid(x)


def _softplus(x):
    return jnp.maximum(x, 0.0) + jnp.log1p(jnp.exp(-jnp.abs(x)))


def _cumsum_rows(x, seg):
    rowi = jnp.bitwise_and(lax.broadcasted_iota(jnp.int32, x.shape, 0), seg - 1)
    s = 1
    while s < seg:
        x = x + jnp.where(rowi >= s, pltpu.roll(x, s, 0), 0.0)
        s *= 2
    return x


def _rmsnorm_kernel(x_ref, g_ref, o_ref):
    x = x_ref[...]
    y = x * lax.rsqrt(jnp.mean(x * x, axis=-1, keepdims=True) + NORM_EPS)
    o_ref[...] = (y * g_ref[...]).astype(o_ref.dtype)


def rmsnorm(x, g, out_dtype):
    M, D = x.shape
    tm = _tile(M, 512)
    return pl.pallas_call(
        _rmsnorm_kernel,
        grid=(M // tm,),
        in_specs=[pl.BlockSpec((tm, D), lambda i: (i, 0)),
                  pl.BlockSpec((1, D), lambda i: (0, 0))],
        out_specs=pl.BlockSpec((tm, D), lambda i: (i, 0)),
        out_shape=jax.ShapeDtypeStruct((M, D), out_dtype),
        compiler_params=_params("parallel"),
        name="rmsnorm",
    )(x, g.reshape(1, D))


def _fold_lanes(v):
    parts = [v[:, k * LANES:(k + 1) * LANES] for k in range(v.shape[1] // LANES)]
    while len(parts) > 1:
        parts = [a + b for a, b in zip(parts[::2], parts[1::2])] + parts[len(parts) & ~1:]
    return parts[0]


def _row_scale(ssq, d):
    r = lax.rsqrt(jnp.sum(ssq, axis=-1, keepdims=True) * (1.0 / d) + NORM_EPS)
    return jnp.broadcast_to(r, ssq.shape)


def _scale_rows(acc, rs_ref):
    return acc * jnp.concatenate([rs_ref[...]] * (acc.shape[1] // LANES), axis=1)


def _norm_prep_kernel(x_ref, g_ref, hb_ref, rs_ref):
    x = x_ref[...]
    hb_ref[...] = (x * g_ref[...]).astype(hb_ref.dtype)
    rs_ref[...] = _row_scale(_fold_lanes(x * x), x.shape[1])


def norm_prep(x, g):
    M, D = x.shape
    tm = _tile(M, 512)
    hb, rs = pl.pallas_call(
        _norm_prep_kernel,
        grid=(M // tm,),
        in_specs=[pl.BlockSpec((tm, D), lambda i: (i, 0)),
                  pl.BlockSpec((1, D), lambda i: (0, 0))],
        out_specs=[pl.BlockSpec((tm, D), lambda i: (i, 0)),
                   pl.BlockSpec((tm, LANES), lambda i: (i, 0))],
        out_shape=[jax.ShapeDtypeStruct((M, D), BF16), jax.ShapeDtypeStruct((M, LANES), F32)],
        compiler_params=_params("parallel"),
        name="norm_prep",
    )(x, g.reshape(1, D))
    return hb, rs


def _w_bf16(w_ref, wb_ref):
    w = w_ref[...]
    if wb_ref is not None:
        w = w.astype(BF16)
        wb_ref[...] = w
    return w


def _w_specs(w, K, n_cols, tn, M, tm):
    arr, row_blk, col_off = w
    emit = arr.dtype != BF16
    assert col_off % tn == 0 and n_cols % tn == 0
    assert not emit or M == tm
    cb = col_off // tn
    in_spec = pl.BlockSpec((K, tn), lambda i, j: (row_blk, cb + j))
    out_specs = [pl.BlockSpec((K, tn), lambda i, j: (0, j))] if emit else []
    out_shapes = [jax.ShapeDtypeStruct((K, n_cols), BF16)] if emit else []
    return arr, emit, in_spec, out_specs, out_shapes


def _x_specs(xn, tm):
    hb, _ = xn
    return [pl.BlockSpec((tm, hb.shape[1]), lambda i, j: (i, 0)),
            pl.BlockSpec((tm, LANES), lambda i, j: (i, 0))]


def _mm_plain_kernel(x_ref, rs_ref, w_ref, *o_refs, emit, silu):
    wb_ref = o_refs[-1] if emit else None
    acc = jnp.dot(x_ref[...], _w_bf16(w_ref, wb_ref), preferred_element_type=F32)
    acc = _scale_rows(acc, rs_ref)
    if silu:
        acc = _silu(acc)
    for o_ref in o_refs[:len(o_refs) - emit]:
        o_ref[...] = acc.astype(o_ref.dtype)


def mm_plain(xn, w, out_dtypes, *, n_cols, tm=1024, tn=1024, silu=False, name="mm_plain"):
    M, K = xn[0].shape
    tm, tn = _tile(M, tm), _tile(n_cols, tn)
    arr, emit, w_spec, wb_specs, wb_shapes = _w_specs(w, K, n_cols, tn, M, tm)
    outs = pl.pallas_call(
        functools.partial(_mm_plain_kernel, emit=emit, silu=silu),
        grid=(M // tm, n_cols // tn),
        in_specs=_x_specs(xn, tm) + [w_spec],
        out_specs=[pl.BlockSpec((tm, tn), lambda i, j: (i, j)) for _ in out_dtypes] + wb_specs,
        out_shape=[jax.ShapeDtypeStruct((M, n_cols), dt) for dt in out_dtypes] + wb_shapes,
        compiler_params=_params("parallel", "arbitrary"),
        name=name,
    )(*xn, arr)
    n = len(out_dtypes)
    return outs[:n], ((outs[n], 0, 0) if emit else w)


def _mm_conv_kernel(x_ref, rs_ref, w_ref, cw_ref, cb_ref, st_ref, o_ref, tail_ref, *rest,
                    emit, seg_len, tiles_per_seq):
    pad_scr, carry_scr = rest[-2:]
    i, j = pl.program_id(0), pl.program_id(1)
    tm, tn = o_ref.shape
    if emit:
        rest[0][...] = w_ref[...].astype(BF16)
        w_ref = rest[0]
    tail_ref[...] = jnp.zeros_like(tail_ref)

    def conv_silu(rows, cols):
        acc = cb_ref[:, cols] + pad_scr[5:5 + rows, cols] * cw_ref[0:1, cols]
        for k in range(1, CONV_W):
            acc = acc + pad_scr[5 + k:5 + k + rows, cols] * cw_ref[k:k + 1, cols]
        return _silu(acc)

    if seg_len == tm:
        prev = st_ref[0]
        if tiles_per_seq > 1:
            @pl.when(i == 0)
            def _():
                carry_scr[j] = jnp.zeros(carry_scr.shape[1:], F32)

            prev = jnp.where(i % tiles_per_seq == 0, prev, carry_scr[j, 0:CONV_W - 1, :])
        pad_scr[5:8, :] = prev
        x = x_ref[...]
        rs = jnp.concatenate([rs_ref[...]] * (CONV_COL_CHUNK // LANES), axis=1)
        for c0 in range(0, tn, CONV_COL_CHUNK):
            cols = slice(c0, c0 + CONV_COL_CHUNK)
            pad_scr[8:8 + tm, cols] = jnp.dot(x, w_ref[:, cols], preferred_element_type=F32) * rs
            o_ref[:, cols] = conv_silu(tm, cols)
        last = pad_scr[5 + tm:8 + tm, :]
        tail_ref[0, 0:CONV_W - 1, :] = last
        if tiles_per_seq > 1:
            carry_scr[j, 0:CONV_W - 1, :] = last
    else:
        raw = _scale_rows(jnp.dot(x_ref[...], w_ref[...], preferred_element_type=F32), rs_ref)
        for s in range(tm // seg_len):
            pad_scr[5:8, :] = st_ref[s]
            pad_scr[8:8 + seg_len, :] = raw[s * seg_len:(s + 1) * seg_len]
            o_ref[s * seg_len:(s + 1) * seg_len, :] = conv_silu(seg_len, slice(0, tn))
            tail_ref[s, 0:CONV_W - 1, :] = pad_scr[5 + seg_len:8 + seg_len, :]


def mm_conv(xn, w, conv_w, conv_b, conv_state, *, T, n_cols, tm=1024, tn=1024):
    M, K = xn[0].shape
    Bsz = M // T
    tm, tn = _tile(M, tm), _tile(n_cols, tn)
    seg_len = min(T, tm)
    assert tm % seg_len == 0 and T % seg_len == 0 and seg_len >= CONV_W - 1
    nseg, tiles_per_seq = tm // seg_len, T // seg_len
    arr, emit, w_spec, wb_specs, wb_shapes = _w_specs(w, K, n_cols, tn, M, tm)
    outs = pl.pallas_call(
        functools.partial(_mm_conv_kernel, emit=emit, seg_len=seg_len, tiles_per_seq=tiles_per_seq),
        grid=(M // tm, n_cols // tn),
        in_specs=_x_specs(xn, tm) + [
            w_spec,
            pl.BlockSpec((CONV_W, tn), lambda i, j: (0, j)),
            pl.BlockSpec((1, tn), lambda i, j: (0, j)),
            pl.BlockSpec((nseg, CONV_W - 1, tn), lambda i, j: (i // tiles_per_seq, 0, j))],
        out_specs=[pl.BlockSpec((tm, tn), lambda i, j: (i, j)),
                   pl.BlockSpec((nseg, 8, tn), lambda i, j: (i, 0, j))] + wb_specs,
        out_shape=[jax.ShapeDtypeStruct((M, n_cols), F32),
                   jax.ShapeDtypeStruct((M // seg_len, 8, n_cols), F32)] + wb_shapes,
        scratch_shapes=[pltpu.VMEM((8 + tm if nseg == 1 else 8 + seg_len, tn), F32),
                        pltpu.VMEM((n_cols // tn, 8, tn), F32)],
        compiler_params=_params("arbitrary", "arbitrary", vmem=CONV_VMEM_LIMIT_BYTES),
        name="mm_conv",
    )(*xn, arr, conv_w, conv_b.reshape(1, n_cols), conv_state)
    tails = outs[1].reshape(Bsz, tiles_per_seq, 8, n_cols)[:, -1, 0:CONV_W - 1, :]
    return outs[0], tails, ((outs[2], 0, 0) if emit else w)


def _mm_resid_kernel(*refs, alpha, n_norm, emit, n_total):
    x_ref, w_ref, r_ref = refs[:3]
    g_refs = refs[3:3 + n_norm]
    o_ref = refs[3 + n_norm]
    hb_refs = refs[4 + n_norm:4 + 2 * n_norm]
    rs_ref = refs[4 + 2 * n_norm] if n_norm else None
    wb_ref = refs[-1] if emit else None
    acc = jnp.dot(x_ref[...], _w_bf16(w_ref, wb_ref), preferred_element_type=F32)
    h = r_ref[...] + alpha * acc
    o_ref[...] = h
    for g_ref, hb_ref in zip(g_refs, hb_refs):
        hb_ref[...] = (h * g_ref[...]).astype(hb_ref.dtype)
    if n_norm:
        part = _fold_lanes(h * h)
        j = pl.program_id(1)

        @pl.when(j == 0)
        def _():
            rs_ref[...] = part

        @pl.when(j > 0)
        def _():
            rs_ref[...] += part

        @pl.when(j == pl.num_programs(1) - 1)
        def _():
            rs_ref[...] = _row_scale(rs_ref[...], n_total)


def mm_resid(x, w, resid, alpha, gains=(), *, tm=512, tn=512, name="mm_resid"):
    M, K = x.shape
    N = resid.shape[1]
    tm, tn = _tile(M, tm), _tile(N, tn)
    n_norm = len(gains)
    arr, emit, w_spec, wb_specs, wb_shapes = _w_specs(w, K, N, tn, M, tm)
    tile = pl.BlockSpec((tm, tn), lambda i, j: (i, j))
    outs = pl.pallas_call(
        functools.partial(_mm_resid_kernel, alpha=alpha, n_norm=n_norm, emit=emit, n_total=N),
        grid=(M // tm, N // tn),
        in_specs=[pl.BlockSpec((tm, K), lambda i, j: (i, 0)), w_spec, tile]
                 + [pl.BlockSpec((1, tn), lambda i, j: (0, j))] * n_norm,
        out_specs=[tile] * (1 + n_norm) + [pl.BlockSpec((tm, LANES), lambda i, j: (i, 0))] * bool(n_norm) + wb_specs,
        out_shape=[jax.ShapeDtypeStruct((M, N), F32)] + [jax.ShapeDtypeStruct((M, N), BF16)] * n_norm
                  + [jax.ShapeDtypeStruct((M, LANES), F32)] * bool(n_norm) + wb_shapes,
        compiler_params=_params("parallel", "arbitrary"),
        name=name,
    )(x, arr, resid, *[g.reshape(1, N) for g in gains])
    xns = [(outs[1 + k], outs[1 + n_norm]) for k in range(n_norm)]
    return outs[0], xns, ((outs[-1], 0, 0) if emit else w)


def _gateup_epilogue(g, u, rs_ref, o_ref):
    o_ref[...] = (_silu(_scale_rows(g, rs_ref)) * _scale_rows(u, rs_ref)).astype(o_ref.dtype)


def _mm_gateup_pair_kernel(x_ref, rs_ref, wg_ref, wu_ref, o_ref, *wgu_refs):
    x = x_ref[...]
    wg, wu = wg_ref[...], wu_ref[...]
    if wgu_refs:
        tn = wg.shape[1]
        wg, wu = wg.astype(BF16), wu.astype(BF16)
        wgu_refs[0][:, 0:tn] = wg
        wgu_refs[0][:, tn:2 * tn] = wu
    g = jnp.dot(x, wg, preferred_element_type=F32)
    u = jnp.dot(x, wu, preferred_element_type=F32)
    _gateup_epilogue(g, u, rs_ref, o_ref)


def _mm_gateup_kernel(x_ref, rs_ref, wgu_hbm, o_ref, wbuf, sem):
    tm, tn = o_ref.shape
    nj = pl.num_programs(1)
    n_steps = pl.num_programs(0) * nj
    g = pl.program_id(0) * nj + pl.program_id(1)
    ahead = W_STREAM_BUFFERS - 1

    def tile_copy(step):
        col = pl.multiple_of((step % nj) * (2 * tn), 2 * tn)
        slot = step % W_STREAM_BUFFERS
        return pltpu.make_async_copy(wgu_hbm.at[:, pl.ds(col, 2 * tn)], wbuf.at[slot], sem.at[slot])

    @pl.when(g == 0)
    def _():
        for s in range(ahead):
            @pl.when(s < n_steps)
            def _():
                tile_copy(s).start()

    @pl.when(g + ahead < n_steps)
    def _():
        tile_copy(g + ahead).start()

    tile_copy(g).wait()
    rows = min(tm, GATEUP_ROW_CHUNK)
    w = wbuf[g % W_STREAM_BUFFERS]
    for r0 in range(0, tm, rows):
        rsl = slice(r0, r0 + rows)
        gu = jnp.dot(x_ref[rsl, :], w, preferred_element_type=F32)
        rs = jnp.concatenate([rs_ref[rsl, :]] * (tn // LANES), axis=1)
        o_ref[rsl, :] = (_silu(gu[:, 0:tn] * rs) * (gu[:, tn:2 * tn] * rs)).astype(o_ref.dtype)


def mm_gateup(xn, wgu, *, n_cols, tm=2048, tn=256):
    M, K = xn[0].shape
    tm, tn = _tile(M, tm), _tile(n_cols, tn)
    out_spec = pl.BlockSpec((tm, tn), lambda i, j: (i, j))
    out_shape = jax.ShapeDtypeStruct((M, n_cols), BF16)
    if isinstance(wgu[0], tuple):
        garr, emit, g_spec, _, _ = _w_specs(wgu[0], K, n_cols, tn, M, tm)
        uarr, _, u_spec, _, _ = _w_specs(wgu[1], K, n_cols, tn, M, tm)
        outs = pl.pallas_call(
            _mm_gateup_pair_kernel,
            grid=(M // tm, n_cols // tn),
            in_specs=_x_specs(xn, tm) + [g_spec, u_spec],
            out_specs=[out_spec] + [pl.BlockSpec((K, 2 * tn), lambda i, j: (0, j))] * emit,
            out_shape=[out_shape] + [jax.ShapeDtypeStruct((K, 2 * n_cols), BF16)] * emit,
            compiler_params=_params("parallel", "arbitrary"),
            name="mm_gateup",
        )(*xn, garr, uarr)
        return outs[0], ((outs[1], 0, 0) if emit else wgu)
    arr, row_blk, col_off = wgu
    assert row_blk == 0 and col_off == 0 and arr.shape == (K, 2 * n_cols)
    out = pl.pallas_call(
        _mm_gateup_kernel,
        grid=(M // tm, n_cols // tn),
        in_specs=_x_specs(xn, tm) + [pl.BlockSpec(memory_space=pl.ANY)],
        out_specs=out_spec,
        out_shape=out_shape,
        scratch_shapes=[pltpu.VMEM((W_STREAM_BUFFERS, K, 2 * tn), BF16),
                        pltpu.SemaphoreType.DMA((W_STREAM_BUFFERS,))],
        compiler_params=_params("arbitrary", "arbitrary"),
        name="mm_gateup",
    )(*xn, arr)
    return out, wgu


def _mm_dt_kernel(x_ref, rs_ref, w_ref, b_ref, alog_ref, dt_ref, acum_ref, *wb_refs, L):
    raw = jnp.dot(x_ref[...], _w_bf16(w_ref, wb_refs[0] if wb_refs else None), preferred_element_type=F32)
    dt = _softplus(_scale_rows(raw, rs_ref) + b_ref[...])
    dt_ref[...] = dt
    acum_ref[...] = _cumsum_rows(dt * (-jnp.exp(alog_ref[...])), L)


def mm_dt(xn, w, dt_bias, a_log, *, L, tm=2048):
    M, K = xn[0].shape
    heads = dt_bias.shape[0]
    tm = _tile(M, tm)
    assert tm % L == 0 and L & (L - 1) == 0
    arr, emit, w_spec, wb_specs, wb_shapes = _w_specs(w, K, heads, heads, M, tm)
    vec = pl.BlockSpec((1, heads), lambda i, j: (0, 0))
    outs = pl.pallas_call(
        functools.partial(_mm_dt_kernel, L=L),
        grid=(M // tm, 1),
        in_specs=_x_specs(xn, tm) + [w_spec, vec, vec],
        out_specs=[pl.BlockSpec((tm, heads), lambda i, j: (i, 0))] * 2 + wb_specs,
        out_shape=[jax.ShapeDtypeStruct((M, heads), F32)] * 2 + wb_shapes,
        compiler_params=_params("parallel", "arbitrary"),
        name="mm_dt",
    )(*xn, arr, dt_bias.reshape(1, heads), a_log.reshape(1, heads))
    return outs[0], outs[1], ((outs[2], 0, 0) if emit else w)


def _mm_logf_kernel(x_ref, rs_ref, w_ref, b_ref, o_ref):
    s = jnp.dot(x_ref[...], w_ref[...], preferred_element_type=F32)
    s = _scale_rows(s, rs_ref) + b_ref[...]
    o_ref[...] = -_softplus(-s)


def mm_logf(xn, w, b):
    M, K = xn[0].shape
    N = w.shape[1]
    tm = _tile(M, 2048)
    return pl.pallas_call(
        _mm_logf_kernel,
        grid=(M // tm, 1),
        in_specs=_x_specs(xn, tm) + [pl.BlockSpec((K, N), lambda i, j: (0, 0)),
                                     pl.BlockSpec((1, N), lambda i, j: (0, 0))],
        out_specs=pl.BlockSpec((tm, N), lambda i, j: (i, 0)),
        out_shape=jax.ShapeDtypeStruct((M, N), F32),
        compiler_params=_params("parallel", "arbitrary"),
        name="mm_logf",
    )(*xn, w, b)


def _split3(v):
    hi = v.astype(BF16)
    r1 = v - hi.astype(F32)
    mid = r1.astype(BF16)
    lo = (r1 - mid.astype(F32)).astype(BF16)
    return [hi, mid, lo]


def _ssd_kernel(gate_ref, x_ref, b_ref, c_ref, dt_ref, acum_ref, dx_ref, ng_ref, e_ref, h0_ref,
                y_ref, hout_ref, ht_scr, *, L, GW):
    c = pl.program_id(2)
    P = SSM_HEAD_DIM
    N = LANES
    n_sub = x_ref.shape[1] // GW

    @pl.when(c == 0)
    def _():
        for s in range(n_sub):
            ht_scr[s] = h0_ref[0, s * GW:(s + 1) * GW, :].T

    lhs = jnp.concatenate(_split3(acum_ref[...]) + _split3(dt_ref[...]), axis=0)

    lane = lax.broadcasted_iota(jnp.int32, (L, LANES), 1)
    rowi = lax.broadcasted_iota(jnp.int32, (L, LANES), 0)
    s_idx = jnp.bitwise_and(lane, P - 1)
    r_idx = jnp.right_shift(lane, P.bit_length() - 1)
    eye2 = rowi == s_idx
    tri2 = s_idx <= rowi
    pad = [] if L == P else [jnp.zeros((P - L, LANES), BF16)]

    def early_dots(s):
        Bb = b_ref[:, s * N:(s + 1) * N].astype(BF16)
        Cb = c_ref[:, s * N:(s + 1) * N].astype(BF16)
        ht = ht_scr[s]
        ex = jnp.dot(lhs, e_ref[s], preferred_element_type=F32)
        ch = jnp.dot(Cb, ht.astype(BF16), preferred_element_type=F32)
        b2 = jnp.concatenate([Bb] + pad + [Bb] + pad, axis=0)
        cb2 = lax.dot_general(Cb, b2, (((1,), (1,)), ((), ())), preferred_element_type=F32)
        return Bb, ht, ex, ch, cb2

    early = {0: early_dots(0)}
    for s in range(n_sub):
        if s + 1 < n_sub:
            early[s + 1] = early_dots(s + 1)
        Bb, ht, ex, ch, cb2 = early.pop(s)
        xs = x_ref[:, s * GW:(s + 1) * GW]
        acum_x = ex[0:L] + ex[L:2 * L] + ex[2 * L:3 * L]
        dt_x = ex[3 * L:4 * L] + ex[4 * L:5 * L] + ex[5 * L:6 * L]

        xdt = xs * dt_x
        y_inter = ch * jnp.exp(acum_x)

        y_parts = []
        for j in range(GW // LANES):
            sl = slice(j * LANES, (j + 1) * LANES)
            ax = acum_x[:, sl]
            v = jnp.sum(jnp.where(eye2, ax, 0.0), axis=0, keepdims=True)
            decay = jnp.exp(jnp.where(tri2, ax - v, -jnp.inf))
            m2 = (cb2 * decay).astype(BF16)
            xp = xdt[:, sl]
            top = jnp.where(r_idx == 0, xp, 0.0).astype(BF16)
            bot = jnp.where(r_idx == 1, xp, 0.0).astype(BF16)
            bd = jnp.concatenate([top] + pad + [bot] + pad, axis=0)
            y_parts.append(jnp.dot(m2, bd, preferred_element_type=F32))
        y = jnp.concatenate(y_parts, axis=1) + y_inter + dx_ref[:, s * GW:(s + 1) * GW] * xs

        y = y * gate_ref[:, s * GW:(s + 1) * GW]
        y = y * lax.rsqrt(jnp.mean(y * y, axis=-1, keepdims=True) + NORM_EPS)
        y_ref[:, s * GW:(s + 1) * GW] = (y * ng_ref[:, s * GW:(s + 1) * GW]).astype(y_ref.dtype)

        last = acum_x[L - 1:L, :]
        xt = (xdt * jnp.exp(last - acum_x)).astype(BF16)
        upd = lax.dot_general(Bb, xt, (((0,), (0,)), ((), ())), preferred_element_type=F32)
        ht_scr[s] = ht * jnp.exp(last) + upd

    @pl.when(c == pl.num_programs(2) - 1)
    def _():
        for s in range(n_sub):
            hout_ref[0, s * GW:(s + 1) * GW, :] = ht_scr[s].T


def ssd_mixer(gate, xbc, dt, acum, ssm_state, d_skip, norm_g, *, Bsz, T, L, d_inner, groups, n_state):
    M = Bsz * T
    heads = dt.shape[1]
    P = SSM_HEAD_DIM
    GW = d_inner // groups
    N = n_state
    R = GW // P
    nc = T // L
    ns = SSD_GROUPS_PER_STEP
    assert N == LANES and heads == LANES and GW % LANES == 0 and L % 16 == 0 and L <= P and groups % ns == 0
    SW, SN = ns * GW, ns * N
    kb, kc = d_inner // SN, (d_inner + groups * N) // SN

    expand = np.zeros((groups, heads, GW), np.float32)
    for g in range(groups):
        for r in range(R):
            expand[g, g * R + r, r * P:(r + 1) * P] = 1.0
    expand = jnp.asarray(expand, BF16)

    d_x = jnp.repeat(d_skip, P).reshape(1, d_inner)
    h0 = ssm_state.reshape(Bsz, heads * P, N)

    row = lambda b, g, c: b * nc + c
    in_specs = [
        pl.BlockSpec((L, SW), lambda b, g, c: (row(b, g, c), g)),
        pl.BlockSpec((L, SW), lambda b, g, c: (row(b, g, c), g)),
        pl.BlockSpec((L, SN), lambda b, g, c: (row(b, g, c), kb + g)),
        pl.BlockSpec((L, SN), lambda b, g, c: (row(b, g, c), kc + g)),
        pl.BlockSpec((L, heads), lambda b, g, c: (row(b, g, c), 0)),
        pl.BlockSpec((L, heads), lambda b, g, c: (row(b, g, c), 0)),
        pl.BlockSpec((1, SW), lambda b, g, c: (0, g)),
        pl.BlockSpec((1, SW), lambda b, g, c: (0, g)),
        pl.BlockSpec((ns, heads, GW), lambda b, g, c: (g, 0, 0)),
        pl.BlockSpec((1, SW, N), lambda b, g, c: (b, g, 0)),
    ]
    out_specs = [
        pl.BlockSpec((L, SW), lambda b, g, c: (row(b, g, c), g)),
        pl.BlockSpec((1, SW, N), lambda b, g, c: (b, g, 0)),
    ]
    out_shape = [
        jax.ShapeDtypeStruct((M, d_inner), BF16),
        jax.ShapeDtypeStruct((Bsz, heads * P, N), F32),
    ]
    y, h_new = pl.pallas_call(
        functools.partial(_ssd_kernel, L=L, GW=GW),
        grid=(Bsz, groups // ns, nc),
        in_specs=in_specs,
        out_specs=out_specs,
        out_shape=out_shape,
        scratch_shapes=[pltpu.VMEM((ns, N, GW), F32)],
        compiler_params=_params("parallel", "parallel", "arbitrary"),
        name="ssd_mixer",
    )(gate, xbc, xbc, xbc, dt, acum, d_x, norm_g.reshape(1, d_inner), expand, h0)
    return y, h_new.reshape(Bsz, heads, P, N)


def _fcum_kernel(past_ref, new_ref, o_ref, w_scr, *, past, Tn, heads):
    w_scr[...] = jnp.zeros_like(w_scr)
    if past:
        w_scr[0:past, 0:heads] = past_ref[0]
    w_scr[past:past + Tn, :] = new_ref[...]
    n = w_scr.shape[0]
    o_ref[0] = _cumsum_rows(w_scr[...], 1 << (n - 1).bit_length()).T


def fcum(cache_logf, logf_new, *, Bsz, Tn):
    past, heads = cache_logf.shape[1], cache_logf.shape[2]
    t_pad = -(-(past + Tn) // LANES) * LANES
    if past == 0:
        cache_logf = jnp.zeros((Bsz, 8, heads), F32)
    return pl.pallas_call(
        functools.partial(_fcum_kernel, past=past, Tn=Tn, heads=heads),
        grid=(Bsz,),
        in_specs=[pl.BlockSpec((1,) + cache_logf.shape[1:], lambda b: (b, 0, 0)),
                  pl.BlockSpec((Tn, LANES), lambda b: (b, 0))],
        out_specs=pl.BlockSpec((1, LANES, t_pad), lambda b: (b, 0, 0)),
        out_shape=jax.ShapeDtypeStruct((Bsz, LANES, t_pad), F32),
        scratch_shapes=[pltpu.VMEM((t_pad, LANES), F32)],
        compiler_params=_params("parallel"),
        name="fcum",
    )(cache_logf, logf_new)


def _row_to_cols(frow):
    return jnp.broadcast_to(frow, (LANES, frow.shape[1])).T


def _attn_prompt_kernel(q_ref, k_ref, v_ref, f_ref, o_ref, vt_scr, fk_scr, *, tb, hd, scale):
    T = q_ref.shape[0]
    nh = q_ref.shape[1] // hd
    nb = T // tb
    h0 = (pl.program_id(1) % (8 // nh)) * nh
    scale = scale * LOG2E
    frows = [f_ref[0, pl.ds(h0 + i, 1), :] * LOG2E for i in range(nh)]
    wide = 2 * tb
    for i in range(nh):
        fk_scr[i] = _row_to_cols(frows[i])
        for j in range(T // wide):
            vt_scr[i, j] = v_ref[j * wide:(j + 1) * wide, i * hd:(i + 1) * hd].astype(F32).T.astype(BF16)
    kofs = lax.broadcasted_iota(jnp.int32, (tb, tb), 0)
    qofs = lax.broadcasted_iota(jnp.int32, (tb, tb), 1)

    for qi in range(nb):
        qs = slice(qi * tb, (qi + 1) * tb)
        q_t = [q_ref[qs, i * hd:(i + 1) * hd].astype(F32).T.astype(BF16) for i in range(nh)]

        def block(start, width, vt, carry, masked):
            qk = [jnp.dot(k_ref[pl.ds(start, width), i * hd:(i + 1) * hd], q_t[i], preferred_element_type=F32)
                  for i in range(nh)]
            out = []
            for i in range(nh):
                m, l, acc = carry[3 * i:3 * i + 3]
                fk = fk_scr[i, pl.ds(start, width), :]
                s = qk[i] * scale
                s = s + frows[i][:, qs] - jnp.concatenate([fk] * (tb // LANES), axis=1)
                if masked:
                    s = jnp.where(kofs <= qofs, s, -jnp.inf)
                m_new = jnp.maximum(m, jnp.max(s, axis=0, keepdims=True))
                alpha = jnp.exp2(m - m_new)
                p = jnp.exp2(s - m_new)
                l = alpha * l + jnp.sum(p, axis=0, keepdims=True)
                acc = alpha * acc + jnp.dot(vt(i), p.astype(BF16), preferred_element_type=F32)
                out += [m_new, l, acc]
            return tuple(out)

        def wide_block(jw, carry):
            return block(jw * wide, wide, lambda i: vt_scr[i, jw], carry, False)

        def narrow_block(j, carry, masked):
            half = slice((j % 2) * tb, (j % 2 + 1) * tb)
            return block(j * tb, tb, lambda i: vt_scr[i, j // 2, :, half], carry, masked)

        carry = (jnp.full((1, tb), -jnp.inf, F32), jnp.zeros((1, tb), F32), jnp.zeros((hd, tb), F32)) * nh
        for jw in range(qi // 2):
            carry = wide_block(jw, carry)
        if qi % 2:
            carry = narrow_block(qi - 1, carry, False)
        carry = narrow_block(qi, carry, True)
        for i in range(nh):
            o_ref[qs, i * hd:(i + 1) * hd] = (carry[3 * i + 2] / carry[3 * i + 1]).T.astype(o_ref.dtype)


def attn_prompt(q, k, v, f_t, *, Bsz, T, heads, hd):
    tb = _tile(T, ATTN_BLOCK)
    nh = ATTN_HEADS_PER_STEP
    assert tb % LANES == 0 and T % (2 * tb) == 0 and heads % 8 == 0 and 8 % nh == 0 and hd == LANES
    blk = pl.BlockSpec((T, nh * hd), lambda b, h: (b, h))
    return pl.pallas_call(
        functools.partial(_attn_prompt_kernel, tb=tb, hd=hd, scale=hd ** -0.5),
        grid=(Bsz, heads // nh),
        in_specs=[blk, blk, blk, pl.BlockSpec((1, 8, T), lambda b, h: (b, h * nh // 8, 0))],
        out_specs=blk,
        out_shape=jax.ShapeDtypeStruct((Bsz * T, heads * hd), BF16),
        scratch_shapes=[pltpu.VMEM((nh, T // (2 * tb), hd, 2 * tb), BF16), pltpu.VMEM((nh, T, LANES), F32)],
        compiler_params=_params("parallel", "parallel"),
        name="attn_prompt",
    )(q, k, v, f_t)


def _attn_sample_kernel(q_ref, kc_ref, vc_ref, kn_ref, vn_ref, f_ref, o_ref, *, Tn, past, hd, nh, scale):
    zrows = jnp.zeros((LANES - Tn, hd), BF16)
    rowi = lax.broadcasted_iota(jnp.int32, (Tn, LANES), 0)
    coli = lax.broadcasted_iota(jnp.int32, (Tn, LANES), 1)
    nt = (((1,), (1,)), ((), ()))
    for hh in range(nh):
        sl = slice(hh * hd, (hh + 1) * hd)
        q = q_ref[:, sl]
        kc = kc_ref[0, :, sl].astype(BF16)
        vc = vc_ref[0, :, sl].astype(BF16)
        kn = jnp.concatenate([kn_ref[:, sl], zrows], axis=0)
        vn = jnp.concatenate([vn_ref[:, sl], zrows], axis=0)
        f_past = f_ref[0, hh:hh + 1, 0:past]
        f_new = f_ref[0, hh:hh + 1, past:past + LANES]
        fq_col = _row_to_cols(f_new)[0:Tn, 0:1]
        s1 = lax.dot_general(q, kc, nt, preferred_element_type=F32) * scale + fq_col - f_past
        s2 = lax.dot_general(q, kn, nt, preferred_element_type=F32) * scale + fq_col - f_new
        s2 = jnp.where(coli <= rowi, s2, -jnp.inf)
        m = jnp.maximum(jnp.max(s1, axis=1, keepdims=True), jnp.max(s2, axis=1, keepdims=True))
        p1 = jnp.exp(s1 - m)
        p2 = jnp.exp(s2 - m)
        l = jnp.sum(p1, axis=1, keepdims=True) + jnp.sum(p2, axis=1, keepdims=True)
        o = jnp.dot(p1.astype(BF16), vc, preferred_element_type=F32)
        o = o + jnp.dot(p2.astype(BF16), vn, preferred_element_type=F32)
        o_ref[:, sl] = (o / l).astype(o_ref.dtype)


def attn_sample(q, cache_k, cache_v, k_new, v_new, f_t, *, Bsz, Tn, heads, hd):
    past = cache_k.shape[1]
    nh = 8
    assert past % LANES == 0 and Tn % 16 == 0 and Tn <= LANES and heads % nh == 0
    ck = cache_k.reshape(Bsz, past, heads * hd)
    cv = cache_v.reshape(Bsz, past, heads * hd)
    return pl.pallas_call(
        functools.partial(_attn_sample_kernel, Tn=Tn, past=past, hd=hd, nh=nh, scale=hd ** -0.5),
        grid=(Bsz, heads // nh),
        in_specs=[pl.BlockSpec((Tn, nh * hd), lambda b, g: (b, g)),
                  pl.BlockSpec((1, past, nh * hd), lambda b, g: (b, 0, g)),
                  pl.BlockSpec((1, past, nh * hd), lambda b, g: (b, 0, g)),
                  pl.BlockSpec((Tn, nh * hd), lambda b, g: (b, g)),
                  pl.BlockSpec((Tn, nh * hd), lambda b, g: (b, g)),
                  pl.BlockSpec((1, nh, f_t.shape[2]), lambda b, g: (b, g, 0))],
        out_specs=pl.BlockSpec((Tn, nh * hd), lambda b, g: (b, g)),
        out_shape=jax.ShapeDtypeStruct((Bsz * Tn, heads * hd), BF16),
        compiler_params=_params("parallel", "parallel"),
        name="attn_sample",
    )(q, ck, cv, k_new, v_new, f_t)


def _trunk(x, conv_st, ssm_st, cache_k, cache_v, cache_logf, w, p):
    Bsz, T, D = x.shape
    M = Bsz * T
    heads, hd = cache_k.shape[2], cache_k.shape[3]
    d_inner = p["a_norm"].shape[-1]
    d_ff = p["d_ff"]
    n_state = ssm_st.shape[-1]
    groups = (conv_st.shape[-1] - d_inner) // (2 * n_state)
    d_conv = d_inner + 2 * groups * n_state
    L = SSD_CHUNK if T % SSD_CHUNK == 0 else T
    small = M <= 512
    wb = {}

    norm_ffn = p["norm_ffn"].reshape(-1, D)
    tn_r = 256 if small else 512

    def ffn(h, xn, idx, next_gains):
        a, wb[f"wgu{idx}"] = mm_gateup(xn, w[f"wgu{idx}"], n_cols=d_ff)
        h, xns, wb[f"wd{idx}"] = mm_resid(a, w[f"wd{idx}"], h, 0.5, next_gains, tn=tn_r, name="mm_down")
        return h, xns

    h = x.reshape(M, D)
    h, (u,) = ffn(h, norm_prep(h, norm_ffn[0]), 0, [p["norm_mix"][0]])
    tn_in = 512 if small else 1024
    (gate,), wb["w_z"] = mm_plain(u, w["w_z"], [F32], n_cols=d_inner, tn=tn_in, silu=True, name="mm_z")
    xbc, new_conv, wb["w_xbc"] = mm_conv(u, w["w_xbc"], p["a_conv_w"][0], p["a_conv_b"][0], conv_st[0],
                                         T=T, n_cols=d_conv, tn=tn_in)
    dt, acum, wb["w_dt"] = mm_dt(u, w["w_dt"], p["a_dt_bias"][0], p["a_A_log"][0], L=L)
    y, new_ssm = ssd_mixer(gate, xbc, dt, acum, ssm_st[0], p["a_D"][0], p["a_norm"][0],
                           Bsz=Bsz, T=T, L=L, d_inner=d_inner, groups=groups, n_state=n_state)
    h, (xn,), wb["w_out"] = mm_resid(y, w["w_out"], h, 1.0, [norm_ffn[1]], tm=1024, tn=256, name="mm_out_proj")
    h, (x_kv, xn) = ffn(h, xn, 1, [p["kv_norm"], norm_ffn[2]])
    tn = 512 if small else 1024
    logf = mm_logf(x_kv, p["w_f"], p["b_f"])
    f_t = fcum(cache_logf, logf, Bsz=Bsz, Tn=T)
    (k32, kb), wb["w_k"] = mm_plain(x_kv, w["w_k"], [F32, BF16], n_cols=D, tn=tn, name="mm_k")
    (v32, vb), wb["w_v"] = mm_plain(x_kv, w["w_v"], [F32, BF16], n_cols=D, tn=tn, name="mm_v")
    h, (u,) = ffn(h, xn, 2, [p["norm_mix"][1]])
    (q,), wb["w_q"] = mm_plain(u, w["w_q"], [BF16], n_cols=D, tn=tn, name="mm_q")
    if cache_k.shape[1] == 0:
        o = attn_prompt(q, kb, vb, f_t, Bsz=Bsz, T=T, heads=heads, hd=hd)
    else:
        o = attn_sample(q, cache_k, cache_v, kb, vb, f_t, Bsz=Bsz, Tn=T, heads=heads, hd=hd)
    h, (xn,), wb["w_o"] = mm_resid(o, w["w_o"], h, 1.0, [norm_ffn[3]], tm=1024, tn=512, name="mm_wo")
    h, _ = ffn(h, xn, 3, [])
    y_out = rmsnorm(h, p["norm_final"], F32).reshape(Bsz, T, D)
    outs = (y_out, new_conv[None], new_ssm[None],
            k32.reshape(Bsz, T, heads, hd), v32.reshape(Bsz, T, heads, hd),
            logf[:, :heads].reshape(Bsz, T, heads))
    return outs, wb


def kernel(x_prompt, x_sample, state_conv, state_ssm, cache_k, cache_v, cache_logf, norm_ffn, ffn_wg, ffn_wu,
           ffn_wd, norm_mix, a_w_in, a_conv_w, a_conv_b, a_dt_bias, a_A_log, a_D, a_norm, a_w_out, kv_norm,
           w_kvf, b_f, b_wq, b_wo, norm_final):
    assert norm_mix.shape[0] == 2 and a_w_in.shape[0] == 1 and b_wq.shape[0] == 1
    D = x_prompt.shape[-1]
    F = ffn_wg.shape[-1]
    heads = cache_k.shape[2]
    d_inner = a_norm.shape[-1]
    d_zxbc = a_w_in.shape[-1] - a_dt_bias.shape[-1]
    p = {"norm_ffn": norm_ffn, "norm_mix": norm_mix, "a_conv_w": a_conv_w, "a_conv_b": a_conv_b,
         "a_dt_bias": a_dt_bias, "a_A_log": a_A_log, "a_D": a_D, "a_norm": a_norm, "kv_norm": kv_norm,
         "norm_final": norm_final, "d_ff": F,
         "w_f": jnp.pad(w_kvf[:, 2 * D:], ((0, 0), (0, LANES - heads))).astype(BF16),
         "b_f": jnp.pad(b_f, (0, LANES - heads)).reshape(1, LANES)}
    wg, wu, wd = ffn_wg.reshape(-1, F), ffn_wu.reshape(-1, F), ffn_wd.reshape(-1, D)
    w = {"w_z": (a_w_in[0], 0, 0), "w_xbc": (a_w_in[0], 0, d_inner), "w_dt": (a_w_in[0], 0, d_zxbc),
         "w_out": (a_w_out[0], 0, 0),
         "w_k": (w_kvf, 0, 0), "w_v": (w_kvf, 0, D), "w_q": (b_wq[0], 0, 0), "w_o": (b_wo[0], 0, 0)}
    for idx in range(wg.shape[0] // D):
        w[f"wgu{idx}"], w[f"wd{idx}"] = ((wg, idx, 0), (wu, idx, 0)), (wd, idx, 0)

    bp = x_prompt.shape[0]
    zero_conv = jnp.zeros((state_conv.shape[0], bp) + state_conv.shape[2:], state_conv.dtype)
    zero_ssm = jnp.zeros((state_ssm.shape[0], bp) + state_ssm.shape[2:], state_ssm.dtype)
    empty_kv = jnp.zeros((bp, 0) + cache_k.shape[2:], cache_k.dtype)
    empty_lf = jnp.zeros((bp, 0, heads), cache_logf.dtype)
    (y_s, s_conv, s_ssm, s_k, s_v, s_logf), wb = _trunk(
        x_sample, state_conv, state_ssm, cache_k, cache_v, cache_logf, w, p)
    (y_p, p_conv, p_ssm, p_k, p_v, p_logf), _ = _trunk(
        x_prompt, zero_conv, zero_ssm, empty_kv, empty_kv, empty_lf, wb, p)
    return (y_p, y_s, p_conv, p_ssm, p_k, p_v, p_logf, s_conv, s_ssm, s_k, s_v, s_logf)
```
